```python
import math
import jax, jax.numpy as jnp
from jax import lax
import numpy as np

D_MODEL = 1024
BATCH = 8
SEQ = 4096
DEPTH = 4

N_META = 16
N_A_LAYERS = DEPTH // 2
N_B_LAYERS = DEPTH - N_A_LAYERS
GDN_HEADS = 8
GDN_DK = D_MODEL // GDN_HEADS
GDN_DV = D_MODEL // GDN_HEADS
GDN_QK_W = GDN_HEADS * GDN_DK
GDN_V_W = GDN_HEADS * GDN_DV
GDN_CONV = 4
CHUNK = 64
DIFF_HEADS = 8
DIFF_DH = D_MODEL // (2 * DIFF_HEADS)
DIFF_W = DIFF_HEADS * 2 * DIFF_DH
Q_BLOCK = 128
D_FF = 11 * D_MODEL // 4
FFN_CONV = 3
EPS = 1e-6

kernel_name = 'yoco_gdn_diffattn_hybrid'


def rms_norm(x, g):
    xf = x.astype(jnp.float32)
    y = xf * lax.rsqrt(jnp.mean(xf * xf, axis=-1, keepdims=True) + EPS)
    return (y * g.astype(jnp.float32)).astype(x.dtype)


def l2norm(t):
    return t * lax.rsqrt(jnp.sum(t * t, axis=-1, keepdims=True) + EPS)


def causal_depthwise_conv(x, w):
    width, c = w.shape
    return lax.conv_general_dilated(x, w[:, None, :].astype(x.dtype), window_strides=(1,), padding=[(width - 1, 0)], dimension_numbers=('NWC', 'WIO', 'NWC'), feature_group_count=c)


def chunked_gated_delta_rule(q, k, v, beta, g):
    bn, lp, h, dk = q.shape
    dv = v.shape[-1]
    n = lp // CHUNK

    def chunks(t):
        return t.reshape(bn, n, CHUNK, h, -1).transpose(1, 0, 3, 2, 4)

    q, k, v = chunks(q), chunks(k), chunks(v)
    beta, g = chunks(beta)[..., 0], chunks(g)[..., 0]
    g_cum = jnp.cumsum(g, axis=-1)
    causal = jnp.tril(jnp.ones((CHUNK, CHUNK), dtype=bool))
    strict = jnp.tril(jnp.ones((CHUNK, CHUNK), dtype=bool), k=-1)
    decay = jnp.exp(jnp.where(causal, g_cum[..., :, None] - g_cum[..., None, :], -jnp.inf))
    k_beta = k * beta[..., None]
    lower = jnp.where(strict, jnp.einsum('nbhcd,nbhsd->nbhcs', k_beta, k) * decay, 0.0)
    rhs = jnp.concatenate([v * beta[..., None], k_beta * jnp.exp(g_cum)[..., None]], axis=-1)
    sol = lax.linalg.triangular_solve(lower + jnp.eye(CHUNK, dtype=lower.dtype), rhs, left_side=True, lower=True, unit_diagonal=True)
    u, w = sol[..., :dv], sol[..., dv:]
    intra = jnp.einsum('nbhcd,nbhsd->nbhcs', q, k) * decay

    def step(state, xs):
        q_c, k_c, u_c, w_c, g_c, a_c = xs
        v_new = u_c - jnp.einsum('bhcd,bhde->bhce', w_c, state)
        o_c = jnp.einsum('bhcd,bhde->bhce', q_c * jnp.exp(g_c)[..., None], state) + jnp.einsum('bhcs,bhse->bhce', a_c, v_new)
        g_last = g_c[..., -1:]
        k_dec = k_c * jnp.exp(g_last - g_c)[..., None]
        state = state * jnp.exp(g_last)[..., None] + jnp.einsum('bhcd,bhce->bhde', k_dec, v_new)
        return state, o_c

    state0 = jnp.zeros((bn, h, dk, dv), dtype=q.dtype)
    _, o = lax.scan(step, state0, (q, k, u, w, g_cum, intra))
    return o.transpose(1, 0, 3, 2, 4).reshape(bn, lp, h, dv)


def gdn_mixer(h, w_in, conv_w, a_log, dt_bias, o_gain, w_o):
    bn, L, _ = h.shape
    H, dk, dv = GDN_HEADS, GDN_DK, GDN_DV
    f32 = jnp.float32
    proj = h @ w_in
    c0 = 2 * GDN_QK_W + GDN_V_W
    qkv = jax.nn.silu(causal_depthwise_conv(proj[..., :c0], conv_w))
    z = proj[..., c0:c0 + GDN_V_W]
    b = proj[..., c0 + GDN_V_W:c0 + GDN_V_W + H]
    a = proj[..., c0 + GDN_V_W + H:]
    q = l2norm(qkv[..., :GDN_QK_W].reshape(bn, L, H, dk).astype(f32)) * (dk ** -0.5)
    k = l2norm(qkv[..., GDN_QK_W:2 * GDN_QK_W].reshape(bn, L, H, dk).astype(f32))
    v = qkv[..., 2 * GDN_QK_W:].reshape(bn, L, H, dv).astype(f32)
    beta = jax.nn.sigmoid(b.astype(f32))
    g = -jnp.exp(a_log.astype(f32)) * jax.nn.softplus(a.astype(f32) + dt_bias.astype(f32))
    pad = CHUNK - N_META

    def padt(t):
        return jnp.pad(t, [(0, 0), (pad, 0)] + [(0, 0)] * (t.ndim - 2))

    o = chunked_gated_delta_rule(padt(q), padt(k), padt(v), padt(beta), padt(g))[:, pad:]
    o = rms_norm(o, o_gain) * jax.nn.silu(z.reshape(bn, L, H, dv).astype(f32))
    return o.reshape(bn, L, GDN_V_W).astype(h.dtype) @ w_o


def alibi_slopes(n_heads):
    return jnp.exp2(-(8.0 / n_heads) * jnp.arange(1, n_heads + 1, dtype=jnp.float32))


def diff_attn_mixer(h, k_shared, v_shared, w_q, lam, subln, lam_init, w_o):
    bn, L, _ = h.shape
    H, dh = DIFF_HEADS, DIFF_DH
    f32 = jnp.float32
    q = (h @ w_q).reshape(bn, L, H, 2, dh)
    nb = -(-L // Q_BLOCK)
    lq = nb * Q_BLOCK
    q = jnp.pad(q, [(0, 0), (0, lq - L), (0, 0), (0, 0), (0, 0)])
    q_blocks = q.reshape(bn, nb, Q_BLOCK, H, 2, dh).transpose(1, 0, 2, 3, 4, 5)
    q_pos = jnp.arange(lq, dtype=jnp.int32).reshape(nb, Q_BLOCK)
    k_pos = jnp.arange(L, dtype=jnp.int32)
    slopes = alibi_slopes(H)[:, None, None, None]
    scale = dh ** -0.5
    vf = v_shared.astype(f32)

    def block(args):
        qb, qp = args
        s = jnp.einsum('bqhid,bkhid->bhiqk', qb, k_shared).astype(f32) * scale
        dist = (qp[:, None] - k_pos[None, :]).astype(f32)
        s = jnp.where(dist >= 0, s - slopes * dist, -jnp.inf)
        p = jax.nn.softmax(s, axis=-1)
        p = p[:, :, 0] - lam * p[:, :, 1]
        return jnp.einsum('bhqk,bkhe->bqhe', p, vf)

    o = lax.map(block, (q_blocks, q_pos))
    o = o.transpose(1, 0, 2, 3, 4).reshape(bn, lq, H, 2 * dh)[:, :L]
    o = rms_norm(o, subln) * (1.0 - lam_init)
    return o.reshape(bn, L, DIFF_W).astype(h.dtype) @ w_o


def conv_ffn(h, w_up, conv_w, w_down):
    u = causal_depthwise_conv(h @ w_up, conv_w)
    gate, up = u[..., :D_FF], u[..., D_FF:]
    return (jax.nn.silu(gate) * up) @ w_down


def setup_inputs(seed: int = 0) -> dict:
    key = jax.random.key(seed)
    ks = jax.random.split(key, 32)
    f32 = jnp.float32
    D = D_MODEL
    nA, nB = N_A_LAYERS, N_B_LAYERS

    def nrm(k, shape, scale):
        return jax.random.normal(k, shape, f32) * scale

    def gain(k, shape):
        return 1.0 + 0.02 * jax.random.normal(k, shape, f32)

    gdn_in_w = 2 * GDN_QK_W + 2 * GDN_V_W + 2 * GDN_HEADS
    dt = jnp.exp(jax.random.uniform(ks[9], (nA, GDN_HEADS), f32, math.log(1e-3), math.log(1e-1)))
    return {
        'x': jax.random.normal(ks[0], (BATCH, SEQ, D), f32),
        'meta_tokens': nrm(ks[1], (N_META, D), 1.0),
        'a_norm': gain(ks[2], (nA, D)),
        'a_w_in': nrm(ks[3], (nA, D, gdn_in_w), D ** -0.5),
        'a_conv': nrm(ks[4], (nA, GDN_CONV, 2 * GDN_QK_W + GDN_V_W), GDN_CONV ** -0.5),
        'a_log': jnp.log(jax.random.uniform(ks[5], (nA, GDN_HEADS), f32, 1.0, 16.0)),
        'a_dt_bias': jnp.log(jnp.expm1(dt)),
        'a_onorm': gain(ks[6], (nA, GDN_DV)),
        'a_w_o': nrm(ks[7], (nA, GDN_V_W, D), GDN_V_W ** -0.5),
        'kv_norm': gain(ks[8], (D,)),
        'w_kv': nrm(ks[10], (D, 2 * DIFF_W), D ** -0.5),
        'lambda_k1': nrm(ks[11], (DIFF_DH,), 0.1),
        'lambda_k2': nrm(ks[12], (DIFF_DH,), 0.1),
        'b_norm': gain(ks[13], (nB, D)),
        'b_w_q': nrm(ks[14], (nB, D, DIFF_W), D ** -0.5),
        'b_lambda_q1': nrm(ks[15], (nB, DIFF_DH), 0.1),
        'b_lambda_q2': nrm(ks[16], (nB, DIFF_DH), 0.1),
        'b_subln': gain(ks[17], (nB, 2 * DIFF_DH)),
        'b_w_o': nrm(ks[18], (nB, DIFF_W, D), DIFF_W ** -0.5),
        'ffn_norm': gain(ks[19], (DEPTH, D)),
        'ffn_w_up': nrm(ks[20], (DEPTH, D, 2 * D_FF), D ** -0.5),
        'ffn_conv': nrm(ks[21], (DEPTH, FFN_CONV, 2 * D_FF), FFN_CONV ** -0.5),
        'ffn_w_down': nrm(ks[22], (DEPTH, D_FF, D), D_FF ** -0.5),
        'final_norm': gain(ks[23], (D,)),
    }


def reference(x, meta_tokens, a_norm, a_w_in, a_conv, a_log, a_dt_bias, a_onorm, a_w_o, kv_norm, w_kv, lambda_k1, lambda_k2, b_norm, b_w_q, b_lambda_q1, b_lambda_q2, b_subln, b_w_o, ffn_norm, ffn_w_up, ffn_conv, ffn_w_down, final_norm):
    bn = x.shape[0]
    meta = jnp.broadcast_to(meta_tokens.astype(x.dtype)[None], (bn, N_META, D_MODEL))
    h = jnp.concatenate([meta, x], axis=1)
    L = h.shape[1]
    k_shared = None
    v_shared = None
    for layer in range(DEPTH):
        if layer < N_A_LAYERS:
            i = layer
            h = h + gdn_mixer(rms_norm(h, a_norm[i]), a_w_in[i], a_conv[i], a_log[i], a_dt_bias[i], a_onorm[i], a_w_o[i])
        else:
            if layer == N_A_LAYERS:
                kv = rms_norm(h, kv_norm) @ w_kv
                k_shared = kv[..., :DIFF_W].reshape(bn, L, DIFF_HEADS, 2, DIFF_DH)
                v_shared = kv[..., DIFF_W:].reshape(bn, L, DIFF_HEADS, 2 * DIFF_DH)
            j = layer - N_A_LAYERS
            lam_init = 0.8 - 0.6 * math.exp(-0.3 * layer)
            lam = (jnp.exp(jnp.sum(b_lambda_q1[j].astype(jnp.float32) * lambda_k1.astype(jnp.float32)))
                   - jnp.exp(jnp.sum(b_lambda_q2[j].astype(jnp.float32) * lambda_k2.astype(jnp.float32))) + lam_init)
            h = h + diff_attn_mixer(rms_norm(h, b_norm[j]), k_shared, v_shared, b_w_q[j], lam, b_subln[j], lam_init, b_w_o[j])
        h = h + conv_ffn(rms_norm(h, ffn_norm[layer]), ffn_w_up[layer], ffn_conv[layer], ffn_w_down[layer])
    return rms_norm(h, final_norm)[:, N_META:]
```

```python
import functools
import math

import jax
import jax.numpy as jnp
from jax import lax
from jax.experimental import pallas as pl
from jax.experimental.pallas import tpu as pltpu

N_META = 16
GDN_HEADS = 8
GDN_CONV = 4
DIFF_HEADS = 8
FFN_CONV = 3
EPS = 1e-6

LANE = 128
PADF = LANE - N_META
CHUNK = 128
ROW_TILE = 384
KV_TILE = 256
FF_TILE = 256
NEG = -1e30
VMEM_LIMIT = 56 * 1024 * 1024

F32 = jnp.float32
BF16 = jnp.bfloat16


def _cparams(n_axes):
    return pltpu.CompilerParams(
        dimension_semantics=("arbitrary",) * n_axes,
        vmem_limit_bytes=VMEM_LIMIT,
    )


def _const_spec(shape):
    zeros = (0,) * len(shape)
    return pl.BlockSpec(shape, lambda *_: zeros, pipeline_mode=pl.Buffered(1))


def _rms(x, g):
    return x * lax.rsqrt(jnp.mean(x * x, axis=-1, keepdims=True) + EPS) * g


def _keep_rows(t, rows):
    r = lax.broadcasted_iota(jnp.int32, (rows, 1), 0)
    return jnp.logical_or(t > 0, r >= PADF)


def _dot(a, b):
    return jnp.dot(a, b, preferred_element_type=F32)


def _dot_nt(a, b):
    return lax.dot_general(a, b, (((1,), (1,)), ((), ())), preferred_element_type=F32)


def _dot_tn(a, b):
    return lax.dot_general(a, b, (((0,), (0,)), ((), ())), preferred_element_type=F32)


def _norm_matmul_kernel(*refs, n_out, scales, n_chunk):
    h_ref, g_ref = refs[0], refs[1]
    w_refs = refs[2:2 + n_out]
    o_refs = refs[2 + n_out:2 + 2 * n_out]
    xn = _rms(h_ref[0], g_ref[...]).astype(BF16)
    for w_ref, o_ref, scale in zip(w_refs, o_refs, scales):
        n = o_ref.shape[-1]
        step = min(n_chunk, n)
        for n0 in range(0, n, step):
            acc = _dot(xn, w_ref[:, n0:n0 + step])
            if scale != 1.0:
                acc = acc * scale
            o_ref[0, :, n0:n0 + step] = acc.astype(o_ref.dtype)


def _norm_matmul(h, g, ws, out_dtypes, scales):
    b, lp, d = h.shape
    tm = ROW_TILE
    kern = functools.partial(_norm_matmul_kernel, n_out=len(ws), scales=tuple(scales), n_chunk=512)
    return pl.pallas_call(
        kern,
        grid=(b, lp // tm),
        in_specs=[pl.BlockSpec((1, tm, d), lambda i, t: (i, t, 0)), _const_spec((1, d))]
        + [_const_spec(w.shape) for w in ws],
        out_specs=[pl.BlockSpec((1, tm, w.shape[1]), lambda i, t: (i, t, 0)) for w in ws],
        out_shape=[jax.ShapeDtypeStruct((b, lp, w.shape[1]), dt) for w, dt in zip(ws, out_dtypes)],
        compiler_params=_cparams(2),
        name="norm_matmul",
    )(h, g.reshape(1, d), *ws)


def _res_matmul_kernel(h_ref, y_ref, w_ref, o_ref):
    t = pl.program_id(1)
    out = h_ref[0] + _dot(y_ref[0], w_ref[...])
    o_ref[0] = jnp.where(_keep_rows(t, out.shape[0]), out, 0.0)


def _res_matmul(h, y, w):
    b, lp, d = h.shape
    tm = ROW_TILE
    return pl.pallas_call(
        _res_matmul_kernel,
        grid=(b, lp // tm),
        in_specs=[pl.BlockSpec((1, tm, d), lambda i, t: (i, t, 0)),
                  pl.BlockSpec((1, tm, y.shape[2]), lambda i, t: (i, t, 0)),
                  _const_spec(w.shape)],
        out_specs=pl.BlockSpec((1, tm, d), lambda i, t: (i, t, 0)),
        out_shape=jax.ShapeDtypeStruct(h.shape, h.dtype),
        input_output_aliases={0: 0},
        compiler_params=_cparams(2),
        name="res_matmul",
    )(h, y, w)


def _tri_inverse(a, row, col):
    n = a.shape[0]
    eye = (row == col).astype(F32)

    def same_block(s):
        sh = s.bit_length() - 1
        return lax.shift_right_logical(row, sh) == lax.shift_right_logical(col, sh)

    nb = jnp.where(same_block(8), -a, 0.0)
    p = eye + nb
    n2 = _dot(nb, nb)
    p = p + _dot(p, n2)
    n4 = _dot(n2, n2)
    p = p + _dot(p, n4)
    s = 8
    while s < n:
        off = jnp.where(jnp.logical_and(same_block(2 * s), jnp.logical_not(same_block(s))), a, 0.0)
        p = p - _dot(p, _dot(off, p))
        s *= 2
    return p


def _gdn_kernel(x_ref, gt_ref, cw_ref, alog_ref, dtb_ref, og_ref, o_ref, state_ref, carry_ref):
    t = pl.program_id(1)
    c = CHUNK
    nh = GDN_HEADS
    dk = LANE
    qk_w = nh * dk

    @pl.when(t == 0)
    def _():
        state_ref[...] = jnp.zeros_like(state_ref)
        carry_ref[...] = jnp.zeros_like(carry_ref)

    x = x_ref[0, :, 0:3 * qk_w].astype(F32)
    ext = jnp.concatenate([carry_ref[...], x], axis=0)
    cw = cw_ref[...]
    conv = (cw[3:4] * x + cw[2:3] * ext[7:7 + c] + cw[1:2] * ext[6:6 + c] + cw[0:1] * ext[5:5 + c])
    carry_ref[...] = x[c - 8:c]
    qkv = conv * jax.nn.sigmoid(conv)

    keep = _keep_rows(t, c)
    gts = gt_ref[0]
    beta_all = jnp.where(keep, jax.nn.sigmoid(gts), 0.0)
    g_all = -jnp.exp(alog_ref[...]) * jax.nn.softplus(gts + dtb_ref[...])
    g_all = jnp.where(keep, g_all, 0.0)

    row = lax.broadcasted_iota(jnp.int32, (c, c), 0)
    col = lax.broadcasted_iota(jnp.int32, (c, c), 1)
    causal = col <= row
    strict = col < row
    tril = causal.astype(F32)
    triu = (row <= col).astype(F32)
    gc_all = jnp.dot(tril, g_all, preferred_element_type=F32, precision=lax.Precision.HIGHEST)
    gct_all = jnp.dot(g_all.T, triu, preferred_element_type=F32, precision=lax.Precision.HIGHEST)

    for h in range(nh):
        qh = qkv[:, h * dk:(h + 1) * dk]
        kh = qkv[:, qk_w + h * dk:qk_w + (h + 1) * dk]
        vh = qkv[:, 2 * qk_w + h * dk:2 * qk_w + (h + 1) * dk]
        qn = qh * lax.rsqrt(jnp.sum(qh * qh, axis=-1, keepdims=True) + EPS) * (dk ** -0.5)
        kn = kh * lax.rsqrt(jnp.sum(kh * kh, axis=-1, keepdims=True) + EPS)
        beta = beta_all[:, h:h + 1]
        gcol = gc_all[:, nh + h:nh + h + 1]
        grow = gct_all[nh + h:nh + h + 1, :]
        decay = jnp.where(causal, jnp.exp(gcol - grow), 0.0)
        kb = kn * beta
        kq = _dot_nt(jnp.concatenate([kb, qn], axis=0), kn)
        a = jnp.where(strict, kq[:c] * decay, 0.0)
        intra = kq[c:] * decay
        tinv = _tri_inverse(a, row, col)
        egc = jnp.exp(gcol)
        uw = _dot(tinv, jnp.concatenate([vh * beta, kb * egc], axis=1))
        u = uw[:, :dk]
        w = uw[:, dk:]
        s = state_ref[h]
        ws_qs = _dot(jnp.concatenate([w, qn * egc], axis=0), s)
        v_new = u - ws_qs[:c]
        o = ws_qs[c:] + _dot(intra, v_new)
        g_last = gcol[c - 1:c]
        k_dec = kn * jnp.exp(g_last - gcol)
        state_ref[h] = s * jnp.exp(g_last) + _dot_tn(k_dec, v_new)
        z = x_ref[0, :, 3 * qk_w + h * dk:3 * qk_w + (h + 1) * dk].astype(F32)
        on = _rms(o, og_ref[...])
        o_ref[0, :, h * dk:(h + 1) * dk] = (on * (z * jax.nn.sigmoid(z))).astype(o_ref.dtype)


def _gdn_core(proj, gates, conv_w, alog_row, dtb_row, o_gain):
    b, lp, width = proj.shape
    v_w = GDN_HEADS * LANE
    c = CHUNK
    return pl.pallas_call(
        _gdn_kernel,
        grid=(b, lp // c),
        in_specs=[pl.BlockSpec((1, c, width), lambda i, t: (i, t, 0)),
                  pl.BlockSpec((1, c, LANE), lambda i, t: (i, t, 0)),
                  _const_spec(conv_w.shape), _const_spec((1, LANE)), _const_spec((1, LANE)),
                  _const_spec((1, LANE))],
        out_specs=pl.BlockSpec((1, c, v_w), lambda i, t: (i, t, 0)),
        out_shape=jax.ShapeDtypeStruct((b, lp, v_w), BF16),
        scratch_shapes=[pltpu.VMEM((GDN_HEADS, LANE, LANE), F32),
                        pltpu.VMEM((8, 3 * v_w), F32)],
        compiler_params=_cparams(2),
        name="gdn_core",
    )(proj, gates, conv_w, alog_row, dtb_row, o_gain.reshape(1, LANE))


def _diff_attn_kernel(q_ref, k_ref, v_ref, slope_ref, lq1_ref, lk1_ref, lq2_ref, lk2_ref, sub_ref,
                      o_ref, m_ref, l_ref, acc_ref, *, lam_init):
    i = pl.program_id(2)
    tq = q_ref.shape[1]
    tk = KV_TILE
    lp = k_ref.shape[1]
    dh = LANE // 2

    q = q_ref[0]
    lane = lax.broadcasted_iota(jnp.int32, (1, LANE), 1)
    zero = jnp.zeros_like(q)
    qq = jnp.concatenate([jnp.where(lane < dh, q, zero), jnp.where(lane >= dh, q, zero)], axis=0)

    m_ref[...] = jnp.full_like(m_ref, NEG)
    l_ref[...] = jnp.zeros_like(l_ref)
    acc_ref[...] = jnp.zeros_like(acc_ref)

    r2 = lax.broadcasted_iota(jnp.int32, (2 * tq, 1), 0)
    qpos = i * tq + jnp.where(r2 >= tq, r2 - tq, r2)
    slope = slope_ref[0]
    slope = jnp.concatenate([slope] * (tk // LANE), axis=1)
    n_kv = ((i + 1) * tq + tk - 1) // tk

    def body(j, carry):
        start = pl.multiple_of(jnp.minimum(j * tk, lp - tk), LANE)
        k = k_ref[0, pl.ds(start, tk), :]
        v = v_ref[0, pl.ds(start, tk), :]
        kpos = start + lax.broadcasted_iota(jnp.int32, (1, tk), 1)
        fresh = jnp.logical_and(kpos >= PADF, kpos >= j * tk)
        bias = jnp.where(fresh, slope * kpos.astype(F32), NEG)
        s = _dot_nt(qq, k) + bias
        s = jnp.where(kpos <= qpos, s, NEG)
        m_prev = m_ref[...]
        m_new = jnp.maximum(m_prev, jnp.max(s, axis=-1, keepdims=True))
        alpha = jnp.exp(m_prev - m_new)
        p = jnp.exp(s - m_new)
        l_ref[...] = alpha * l_ref[...] + jnp.sum(p, axis=-1, keepdims=True)
        acc_ref[...] = alpha * acc_ref[...] + _dot(p.astype(BF16), v)
        m_ref[...] = m_new
        return carry

    lax.fori_loop(0, n_kv, body, 0)

    o = acc_ref[...] / l_ref[...]
    lam = (jnp.exp(jnp.sum(lq1_ref[...] * lk1_ref[...], axis=-1, keepdims=True))
           - jnp.exp(jnp.sum(lq2_ref[...] * lk2_ref[...], axis=-1, keepdims=True)) + lam_init)
    od = o[:tq] - lam * o[tq:]
    o_ref[0] = (_rms(od, sub_ref[...]) * (1.0 - lam_init)).astype(o_ref.dtype)


def _diff_attn(q, kv, slopes, lq1, lk1, lq2, lk2, subln, lam_init):
    b, lp, width = q.shape
    nh = DIFF_HEADS
    tq = ROW_TILE
    dh = lq1.shape[-1]
    kern = functools.partial(_diff_attn_kernel, lam_init=lam_init)
    vec = lambda a: a.reshape(1, dh).astype(F32)
    return pl.pallas_call(
        kern,
        grid=(b, nh, lp // tq),
        in_specs=[pl.BlockSpec((1, tq, LANE), lambda bi, h, i: (bi, i, h)),
                  pl.BlockSpec((1, lp, LANE), lambda bi, h, i: (bi, 0, h)),
                  pl.BlockSpec((1, lp, LANE), lambda bi, h, i: (bi, 0, nh + h)),
                  pl.BlockSpec((1, 1, LANE), lambda bi, h, i: (h, 0, 0)),
                  _const_spec((1, dh)), _const_spec((1, dh)), _const_spec((1, dh)), _const_spec((1, dh)),
                  _const_spec((1, LANE))],
        out_specs=pl.BlockSpec((1, tq, LANE), lambda bi, h, i: (bi, i, h)),
        out_shape=jax.ShapeDtypeStruct((b, lp, width), BF16),
        scratch_shapes=[pltpu.VMEM((2 * tq, 1), F32), pltpu.VMEM((2 * tq, 1), F32),
                        pltpu.VMEM((2 * tq, LANE), F32)],
        compiler_params=_cparams(3),
        name="diff_attn",
    )(q, kv, kv, slopes, vec(lq1), vec(lk1), vec(lq2), vec(lk2), subln.reshape(1, LANE).astype(F32))


def _ffn_kernel(h_ref, g_ref, wg_ref, wu_ref, cg_ref, cu_ref, wd_ref, o_ref,
                xn_ref, acc_ref, carry_g_ref, carry_u_ref):
    t = pl.program_id(1)
    tm = h_ref.shape[1]
    nf = wg_ref.shape[0]

    @pl.when(t == 0)
    def _():
        carry_g_ref[...] = jnp.zeros_like(carry_g_ref)
        carry_u_ref[...] = jnp.zeros_like(carry_u_ref)

    x = h_ref[0]
    xn_ref[...] = _rms(x, g_ref[...]).astype(BF16)
    acc_ref[...] = jnp.zeros_like(acc_ref)

    def conv(u, carry_ref, cw, f):
        ext = jnp.concatenate([carry_ref[f], u], axis=0)
        carry_ref[f] = u[tm - 8:tm]
        return cw[2:3] * u + cw[1:2] * ext[7:7 + tm] + cw[0:1] * ext[6:6 + tm]

    def body(f, carry):
        xn = xn_ref[...]
        gate = conv(_dot(xn, wg_ref[f]), carry_g_ref, cg_ref[f], f)
        up = conv(_dot(xn, wu_ref[f]), carry_u_ref, cu_ref[f], f)
        act = gate * jax.nn.sigmoid(gate) * up
        acc_ref[...] += _dot(act.astype(BF16), wd_ref[f])
        return carry

    lax.fori_loop(0, nf, body, 0)
    out = x + acc_ref[...]
    o_ref[0] = jnp.where(_keep_rows(t, tm), out, 0.0)


def _ffn(h, g, wg, wu, cg, cu, wd):
    b, lp, d = h.shape
    tm = ROW_TILE
    nf, _, tf = wg.shape
    return pl.pallas_call(
        _ffn_kernel,
        grid=(b, lp // tm),
        in_specs=[pl.BlockSpec((1, tm, d), lambda i, t: (i, t, 0)), _const_spec((1, d)),
                  _const_spec(wg.shape), _const_spec(wu.shape), _const_spec(cg.shape),
                  _const_spec(cu.shape), _const_spec(wd.shape)],
        out_specs=pl.BlockSpec((1, tm, d), lambda i, t: (i, t, 0)),
        out_shape=jax.ShapeDtypeStruct(h.shape, h.dtype),
        scratch_shapes=[pltpu.VMEM((tm, d), BF16), pltpu.VMEM((tm, d), F32),
                        pltpu.VMEM((nf, 8, tf), F32), pltpu.VMEM((nf, 8, tf), F32)],
        input_output_aliases={0: 0},
        compiler_params=_cparams(2),
        name="conv_ffn",
    )(h, g.reshape(1, d), wg, wu, cg, cu, wd)


def _final_norm_kernel(h_ref, g_ref, o_ref):
    o_ref[0] = _rms(h_ref[0], g_ref[...])


def _final_norm(h, g, seq):
    b, lp, d = h.shape
    tm = LANE
    return pl.pallas_call(
        _final_norm_kernel,
        grid=(b, seq // tm),
        in_specs=[pl.BlockSpec((1, tm, d), lambda i, t: (i, t + 1, 0)), _const_spec((1, d))],
        out_specs=pl.BlockSpec((1, tm, d), lambda i, t: (i, t, 0)),
        out_shape=jax.ShapeDtypeStruct((b, seq, d), h.dtype),
        compiler_params=_cparams(2),
        name="final_norm",
    )(h, g.reshape(1, d))


def _lane_row(vals, offset):
    return jnp.zeros((1, LANE), F32).at[0, offset:offset + vals.shape[0]].set(vals.astype(F32))


def kernel(x, meta_tokens, a_norm, a_w_in, a_conv, a_log, a_dt_bias, a_onorm, a_w_o, kv_norm, w_kv, lambda_k1, lambda_k2, b_norm, b_w_q, b_lambda_q1, b_lambda_q2, b_subln, b_w_o, ffn_norm, ffn_w_up, ffn_conv, ffn_w_down, final_norm):
    bn, seq, d = x.shape
    n_a = a_w_in.shape[0]
    n_b = b_w_q.shape[0]
    depth = n_a + n_b
    nh = GDN_HEADS
    qkvz_w = 4 * nh * LANE
    assert d == nh * LANE and seq % LANE == 0 and (LANE + seq) % ROW_TILE == 0
    d_ff = ffn_w_down.shape[1]
    assert d_ff % FF_TILE == 0
    nf = d_ff // FF_TILE

    meta = jnp.broadcast_to(meta_tokens.astype(x.dtype)[None], (bn, N_META, d))
    h = jnp.concatenate([jnp.zeros((bn, PADF, d), x.dtype), meta, x], axis=1)

    def ffn_layer(h, layer):
        w_up = ffn_w_up[layer].astype(BF16).reshape(d, 2, nf, FF_TILE).transpose(1, 2, 0, 3)
        cw = ffn_conv[layer].astype(F32).reshape(FFN_CONV, 2, nf, FF_TILE).transpose(1, 2, 0, 3)
        w_dn = ffn_w_down[layer].astype(BF16).reshape(nf, FF_TILE, d)
        return _ffn(h, ffn_norm[layer], w_up[0], w_up[1], cw[0], cw[1], w_dn)

    for layer in range(n_a):
        w_in = a_w_in[layer]
        w_main = w_in[:, :qkvz_w].astype(BF16)
        w_gate = jnp.pad(w_in[:, qkvz_w:], ((0, 0), (0, LANE - 2 * nh))).astype(BF16)
        proj, gates = _norm_matmul(h, a_norm[layer], [w_main, w_gate], [BF16, F32], [1.0, 1.0])
        o = _gdn_core(proj, gates, a_conv[layer].astype(F32), _lane_row(a_log[layer], nh),
                      _lane_row(a_dt_bias[layer], nh), a_onorm[layer].astype(F32))
        h = _res_matmul(h, o, a_w_o[layer].astype(BF16))
        h = ffn_layer(h, layer)

    (kv,) = _norm_matmul(h, kv_norm, [w_kv.astype(BF16)], [BF16], [1.0])
    slopes = jnp.exp2(-(8.0 / DIFF_HEADS) * jnp.arange(1, DIFF_HEADS + 1, dtype=F32))
    slopes = jnp.broadcast_to(slopes[:, None, None], (DIFF_HEADS, 1, LANE))
    dh = lambda_k1.shape[0]
    for j in range(n_b):
        layer = n_a + j
        lam_init = 0.8 - 0.6 * math.exp(-0.3 * layer)
        (q,) = _norm_matmul(h, b_norm[j], [b_w_q[j].astype(BF16)], [BF16], [dh ** -0.5])
        o = _diff_attn(q, kv, slopes, b_lambda_q1[j], lambda_k1, b_lambda_q2[j], lambda_k2,
                       b_subln[j], lam_init)
        h = _res_matmul(h, o, b_w_o[j].astype(BF16))
        h = ffn_layer(h, layer)

    return _final_norm(h, final_norm, seq)
```

```python
import functools
import math

import jax
import jax.numpy as jnp
from jax import lax
from jax.experimental import pallas as pl
from jax.experimental.pallas import tpu as pltpu

N_META = 16
GDN_HEADS = 8
GDN_CONV = 4
DIFF_HEADS = 8
FFN_CONV = 3
EPS = 1e-6

LANE = 128
PADF = LANE - N_META
CHUNK = 128
ROW_TILE = 384
KV_TILE = 256
Q_CHUNK = 256
FF_TILE = 256
NEG = -1e30
VMEM_LIMIT = 56 * 1024 * 1024

F32 = jnp.float32
BF16 = jnp.bfloat16


def _cparams(n_axes):
    return pltpu.CompilerParams(
        dimension_semantics=("arbitrary",) * n_axes,
        vmem_limit_bytes=VMEM_LIMIT,
    )


def _const_spec(shape):
    zeros = (0,) * len(shape)
    return pl.BlockSpec(shape, lambda *_: zeros, pipeline_mode=pl.Buffered(1))


def _rms(x, g):
    return x * lax.rsqrt(jnp.mean(x * x, axis=-1, keepdims=True) + EPS) * g


def _keep_rows(t, rows):
    r = lax.broadcasted_iota(jnp.int32, (rows, 1), 0)
    return jnp.logical_or(t > 0, r >= PADF)


def _dot(a, b):
    return jnp.dot(a, b, preferred_element_type=F32)


def _dot_nt(a, b):
    return lax.dot_general(a, b, (((1,), (1,)), ((), ())), preferred_element_type=F32)


def _dot_tn(a, b):
    return lax.dot_general(a, b, (((0,), (0,)), ((), ())), preferred_element_type=F32)


def _norm_matmul_kernel(*refs, n_out, scales, transposed, n_chunk):
    h_ref, g_ref = refs[0], refs[1]
    w_refs = refs[2:2 + n_out]
    o_refs = refs[2 + n_out:2 + 2 * n_out]
    xn = _rms(h_ref[0], g_ref[...]).astype(BF16)
    for w_ref, o_ref, scale, tr in zip(w_refs, o_refs, scales, transposed):
        n = w_ref.shape[-1]
        step = min(n_chunk, n)
        for n0 in range(0, n, step):
            acc = _dot(xn, w_ref[:, n0:n0 + step])
            if scale != 1.0:
                acc = acc * scale
            if tr:
                o_ref[0, n0:n0 + step, :] = acc.T.astype(o_ref.dtype)
            else:
                o_ref[0, :, n0:n0 + step] = acc.astype(o_ref.dtype)


def _norm_matmul(h, g, ws, out_dtypes, scales, transposed=None):
    b, lp, d = h.shape
    tm = ROW_TILE
    transposed = tuple(transposed) if transposed is not None else (False,) * len(ws)
    kern = functools.partial(_norm_matmul_kernel, n_out=len(ws), scales=tuple(scales),
                             transposed=transposed, n_chunk=512)
    out_specs, out_shape = [], []
    for w, dt, tr in zip(ws, out_dtypes, transposed):
        n = w.shape[1]
        if tr:
            out_specs.append(pl.BlockSpec((1, n, tm), lambda i, t: (i, 0, t)))
            out_shape.append(jax.ShapeDtypeStruct((b, n, lp), dt))
        else:
            out_specs.append(pl.BlockSpec((1, tm, n), lambda i, t: (i, t, 0)))
            out_shape.append(jax.ShapeDtypeStruct((b, lp, n), dt))
    return pl.pallas_call(
        kern,
        grid=(b, lp // tm),
        in_specs=[pl.BlockSpec((1, tm, d), lambda i, t: (i, t, 0)), _const_spec((1, d))]
        + [_const_spec(w.shape) for w in ws],
        out_specs=out_specs,
        out_shape=out_shape,
        compiler_params=_cparams(2),
        name="norm_matmul",
    )(h, g.reshape(1, d), *ws)


def _res_matmul_kernel(h_ref, y_ref, w_ref, o_ref):
    t = pl.program_id(1)
    out = h_ref[0] + _dot(y_ref[0], w_ref[...])
    o_ref[0] = jnp.where(_keep_rows(t, out.shape[0]), out, 0.0)


def _res_matmul(h, y, w):
    b, lp, d = h.shape
    tm = ROW_TILE
    return pl.pallas_call(
        _res_matmul_kernel,
        grid=(b, lp // tm),
        in_specs=[pl.BlockSpec((1, tm, d), lambda i, t: (i, t, 0)),
                  pl.BlockSpec((1, tm, y.shape[2]), lambda i, t: (i, t, 0)),
                  _const_spec(w.shape)],
        out_specs=pl.BlockSpec((1, tm, d), lambda i, t: (i, t, 0)),
        out_shape=jax.ShapeDtypeStruct(h.shape, h.dtype),
        input_output_aliases={0: 0},
        compiler_params=_cparams(2),
        name="res_matmul",
    )(h, y, w)


def _tri_inverse(a, row, col):
    n = a.shape[0]
    eye = (row == col).astype(F32)

    def same_block(s):
        sh = s.bit_length() - 1
        return lax.shift_right_logical(row, sh) == lax.shift_right_logical(col, sh)

    nb = jnp.where(same_block(8), -a, 0.0)
    p = eye + nb
    n2 = _dot(nb, nb)
    p = p + _dot(p, n2)
    n4 = _dot(n2, n2)
    p = p + _dot(p, n4)
    s = 8
    while s < n:
        off = jnp.where(jnp.logical_and(same_block(2 * s), jnp.logical_not(same_block(s))), a, 0.0)
        p = p - _dot(p, _dot(off, p))
        s *= 2
    return p


def _gdn_kernel(x_ref, gt_ref, cw_ref, alog_ref, dtb_ref, og_ref, o_ref, state_ref, carry_ref):
    t = pl.program_id(1)
    c = CHUNK
    nh = GDN_HEADS
    dk = LANE
    qk_w = nh * dk

    @pl.when(t == 0)
    def _():
        state_ref[...] = jnp.zeros_like(state_ref)
        carry_ref[...] = jnp.zeros_like(carry_ref)

    x = x_ref[0, :, 0:3 * qk_w].astype(F32)
    ext = jnp.concatenate([carry_ref[...], x], axis=0)
    cw = cw_ref[...]
    conv = (cw[3:4] * x + cw[2:3] * ext[7:7 + c] + cw[1:2] * ext[6:6 + c] + cw[0:1] * ext[5:5 + c])
    carry_ref[...] = x[c - 8:c]
    qkv = conv * jax.nn.sigmoid(conv)

    keep = _keep_rows(t, c)
    gts = gt_ref[0]
    beta_all = jnp.where(keep, jax.nn.sigmoid(gts), 0.0)
    g_all = -jnp.exp(alog_ref[...]) * jax.nn.softplus(gts + dtb_ref[...])
    g_all = jnp.where(keep, g_all, 0.0)

    row = lax.broadcasted_iota(jnp.int32, (c, c), 0)
    col = lax.broadcasted_iota(jnp.int32, (c, c), 1)
    causal = col <= row
    strict = col < row
    tril = causal.astype(F32)
    triu = (row <= col).astype(F32)
    gc_all = jnp.dot(tril, g_all, preferred_element_type=F32, precision=lax.Precision.HIGHEST)
    gct_all = jnp.dot(g_all.T, triu, preferred_element_type=F32, precision=lax.Precision.HIGHEST)

    for h in range(nh):
        qh = qkv[:, h * dk:(h + 1) * dk]
        kh = qkv[:, qk_w + h * dk:qk_w + (h + 1) * dk]
        vh = qkv[:, 2 * qk_w + h * dk:2 * qk_w + (h + 1) * dk]
        qn = qh * lax.rsqrt(jnp.sum(qh * qh, axis=-1, keepdims=True) + EPS) * (dk ** -0.5)
        kn = kh * lax.rsqrt(jnp.sum(kh * kh, axis=-1, keepdims=True) + EPS)
        beta = beta_all[:, h:h + 1]
        gcol = gc_all[:, nh + h:nh + h + 1]
        grow = gct_all[nh + h:nh + h + 1, :]
        decay = jnp.where(causal, jnp.exp(gcol - grow), 0.0)
        kb = kn * beta
        kq = _dot_nt(jnp.concatenate([kb, qn], axis=0), kn)
        a = jnp.where(strict, kq[:c] * decay, 0.0)
        intra = kq[c:] * decay
        tinv = _tri_inverse(a, row, col)
        egc = jnp.exp(gcol)
        uw = _dot(tinv, jnp.concatenate([vh * beta, kb * egc], axis=1))
        u = uw[:, :dk]
        w = uw[:, dk:]
        s = state_ref[h]
        ws_qs = _dot(jnp.concatenate([w, qn * egc], axis=0), s)
        v_new = u - ws_qs[:c]
        o = ws_qs[c:] + _dot(intra, v_new)
        g_last = gcol[c - 1:c]
        k_dec = kn * jnp.exp(g_last - gcol)
        state_ref[h] = s * jnp.exp(g_last) + _dot_tn(k_dec, v_new)
        z = x_ref[0, :, 3 * qk_w + h * dk:3 * qk_w + (h + 1) * dk].astype(F32)
        on = _rms(o, og_ref[...])
        o_ref[0, :, h * dk:(h + 1) * dk] = (on * (z * jax.nn.sigmoid(z))).astype(o_ref.dtype)


def _gdn_core(proj, gates, conv_w, alog_row, dtb_row, o_gain):
    b, lp, width = proj.shape
    v_w = GDN_HEADS * LANE
    c = CHUNK
    return pl.pallas_call(
        _gdn_kernel,
        grid=(b, lp // c),
        in_specs=[pl.BlockSpec((1, c, width), lambda i, t: (i, t, 0)),
                  pl.BlockSpec((1, c, LANE), lambda i, t: (i, t, 0)),
                  _const_spec(conv_w.shape), _const_spec((1, LANE)), _const_spec((1, LANE)),
                  _const_spec((1, LANE))],
        out_specs=pl.BlockSpec((1, c, v_w), lambda i, t: (i, t, 0)),
        out_shape=jax.ShapeDtypeStruct((b, lp, v_w), BF16),
        scratch_shapes=[pltpu.VMEM((GDN_HEADS, LANE, LANE), F32),
                        pltpu.VMEM((8, 3 * v_w), F32)],
        compiler_params=_cparams(2),
        name="gdn_core",
    )(proj, gates, conv_w, alog_row, dtb_row, o_gain.reshape(1, LANE))


def _diff_attn_kernel(q_ref, k_ref, vt_ref, slope_ref, lq1_ref, lk1_ref, lq2_ref, lk2_ref, sub_ref,
                      o_ref, m_ref, l_ref, acc_ref, bias_ref, *, lam_init):
    i = pl.program_id(2)
    tq = q_ref.shape[1]
    tk = KV_TILE
    lp = k_ref.shape[1]
    dh = LANE // 2
    qc = Q_CHUNK

    q = q_ref[0]
    lane = lax.broadcasted_iota(jnp.int32, (1, LANE), 1)
    zero = jnp.zeros_like(q)
    qq = jnp.concatenate([jnp.where(lane < dh, q, zero), jnp.where(lane >= dh, q, zero)], axis=0)

    m_ref[...] = jnp.full_like(m_ref, NEG)
    l_ref[...] = jnp.zeros_like(l_ref)
    acc_ref[...] = jnp.zeros_like(acc_ref)

    slope = slope_ref[0][:, 0:1]
    krel = lax.broadcasted_iota(jnp.int32, (tk, qc), 0)
    bias_ref[...] = slope * krel.astype(F32)
    c2 = lax.broadcasted_iota(jnp.int32, (1, 2 * tq), 1)
    qpos = i * tq + jnp.where(c2 >= tq, c2 - tq, c2)

    def block_start(j):
        return pl.multiple_of(jnp.minimum(j * tk, lp - tk), LANE)

    def scores(j):
        return _dot_nt(k_ref[0, pl.ds(block_start(j), tk), :], qq)

    def step(j, s_all, masked):
        s_next = scores(j + 1)
        start = block_start(j)
        vt = vt_ref[0, :, pl.ds(start, tk)]
        c = slope * jnp.asarray(start, F32)
        bias = bias_ref[...]
        m_prev = m_ref[...]
        l_prev = l_ref[...]
        m_new, alpha, l_new, pv = [], [], [], []
        for c0 in range(0, 2 * tq, qc):
            cs = slice(c0, c0 + qc)
            s = s_all[:, cs] + bias
            if masked:
                kpos = start + krel
                valid = jnp.logical_and(kpos <= qpos[:, cs], kpos >= jnp.maximum(j * tk, PADF))
                s = jnp.where(valid, s, NEG)
            m_c = jnp.maximum(m_prev[:, cs], jnp.max(s, axis=0, keepdims=True) + c)
            a_c = jnp.exp(m_prev[:, cs] - m_c)
            p = jnp.exp(s - (m_c - c))
            m_new.append(m_c)
            alpha.append(a_c)
            l_new.append(a_c * l_prev[:, cs] + jnp.sum(p, axis=0, keepdims=True))
            pv.append(_dot(vt, p.astype(BF16)))
        m_ref[...] = jnp.concatenate(m_new, axis=1)
        l_ref[...] = jnp.concatenate(l_new, axis=1)
        acc_ref[...] = jnp.concatenate(alpha, axis=1) * acc_ref[...] + jnp.concatenate(pv, axis=1)
        return s_next

    n_kv = ((i + 1) * tq + tk - 1) // tk
    n_plain_end = jnp.maximum((i * tq + 1) // tk, 1)
    s_all = step(0, scores(0), True)
    s_all = lax.fori_loop(1, n_plain_end, lambda j, s: step(j, s, False), s_all)
    lax.fori_loop(n_plain_end, n_kv, lambda j, s: step(j, s, True), s_all)

    o = acc_ref[...] / l_ref[...]
    lam = (jnp.exp(jnp.sum(lq1_ref[...] * lk1_ref[...], axis=-1, keepdims=True))
           - jnp.exp(jnp.sum(lq2_ref[...] * lk2_ref[...], axis=-1, keepdims=True)) + lam_init)
    od = o[:, :tq] - lam * o[:, tq:]
    on = od * lax.rsqrt(jnp.mean(od * od, axis=0, keepdims=True) + EPS) * sub_ref[...]
    o_ref[0] = (on * (1.0 - lam_init)).T.astype(o_ref.dtype)


def _diff_attn(q, k, vt, slopes, lq1, lk1, lq2, lk2, subln, lam_init):
    b, lp, width = q.shape
    nh = DIFF_HEADS
    tq = ROW_TILE
    dh = lq1.shape[-1]
    kern = functools.partial(_diff_attn_kernel, lam_init=lam_init)
    vec = lambda a: a.reshape(1, dh).astype(F32)
    return pl.pallas_call(
        kern,
        grid=(b, nh, lp // tq),
        in_specs=[pl.BlockSpec((1, tq, LANE), lambda bi, h, i: (bi, i, h)),
                  pl.BlockSpec((1, lp, LANE), lambda bi, h, i: (bi, 0, h)),
                  pl.BlockSpec((1, LANE, lp), lambda bi, h, i: (bi, h, 0)),
                  pl.BlockSpec((1, 1, LANE), lambda bi, h, i: (h, 0, 0)),
                  _const_spec((1, dh)), _const_spec((1, dh)), _const_spec((1, dh)), _const_spec((1, dh)),
                  _const_spec((LANE, 1))],
        out_specs=pl.BlockSpec((1, tq, LANE), lambda bi, h, i: (bi, i, h)),
        out_shape=jax.ShapeDtypeStruct((b, lp, width), BF16),
        scratch_shapes=[pltpu.VMEM((1, 2 * tq), F32), pltpu.VMEM((1, 2 * tq), F32),
                        pltpu.VMEM((LANE, 2 * tq), F32), pltpu.VMEM((KV_TILE, Q_CHUNK), F32)],
        compiler_params=_cparams(3),
        name="diff_attn",
    )(q, k, vt, slopes, vec(lq1), vec(lk1), vec(lq2), vec(lk2), subln.reshape(LANE, 1).astype(F32))


def _ffn_kernel(h_ref, g_ref, wg_ref, wu_ref, cg_ref, cu_ref, wd_ref, o_ref,
                xn_ref, acc_ref, carry_g_ref, carry_u_ref):
    t = pl.program_id(1)
    tm = h_ref.shape[1]
    nf = wg_ref.shape[0]

    @pl.when(t == 0)
    def _():
        carry_g_ref[...] = jnp.zeros_like(carry_g_ref)
        carry_u_ref[...] = jnp.zeros_like(carry_u_ref)

    x = h_ref[0]
    xn_ref[...] = _rms(x, g_ref[...]).astype(BF16)
    acc_ref[...] = jnp.zeros_like(acc_ref)

    def conv(u, carry_ref, cw, f):
        ext = jnp.concatenate([carry_ref[f], u], axis=0)
        carry_ref[f] = u[tm - 8:tm]
        return cw[2:3] * u + cw[1:2] * ext[7:7 + tm] + cw[0:1] * ext[6:6 + tm]

    def body(f, carry):
        xn = xn_ref[...]
        gate = conv(_dot(xn, wg_ref[f]), carry_g_ref, cg_ref[f], f)
        up = conv(_dot(xn, wu_ref[f]), carry_u_ref, cu_ref[f], f)
        act = gate * jax.nn.sigmoid(gate) * up
        acc_ref[...] += _dot(act.astype(BF16), wd_ref[f])
        return carry

    lax.fori_loop(0, nf, body, 0)
    out = x + acc_ref[...]
    o_ref[0] = jnp.where(_keep_rows(t, tm), out, 0.0)


def _ffn(h, g, wg, wu, cg, cu, wd):
    b, lp, d = h.shape
    tm = ROW_TILE
    nf, _, tf = wg.shape
    return pl.pallas_call(
        _ffn_kernel,
        grid=(b, lp // tm),
        in_specs=[pl.BlockSpec((1, tm, d), lambda i, t: (i, t, 0)), _const_spec((1, d)),
                  _const_spec(wg.shape), _const_spec(wu.shape), _const_spec(cg.shape),
                  _const_spec(cu.shape), _const_spec(wd.shape)],
        out_specs=pl.BlockSpec((1, tm, d), lambda i, t: (i, t, 0)),
        out_shape=jax.ShapeDtypeStruct(h.shape, h.dtype),
        scratch_shapes=[pltpu.VMEM((tm, d), BF16), pltpu.VMEM((tm, d), F32),
                        pltpu.VMEM((nf, 8, tf), F32), pltpu.VMEM((nf, 8, tf), F32)],
        input_output_aliases={0: 0},
        compiler_params=_cparams(2),
        name="conv_ffn",
    )(h, g.reshape(1, d), wg, wu, cg, cu, wd)


def _final_norm_kernel(h_ref, g_ref, o_ref):
    o_ref[0] = _rms(h_ref[0], g_ref[...])


def _final_norm(h, g, seq):
    b, lp, d = h.shape
    tm = LANE
    return pl.pallas_call(
        _final_norm_kernel,
        grid=(b, seq // tm),
        in_specs=[pl.BlockSpec((1, tm, d), lambda i, t: (i, t + 1, 0)), _const_spec((1, d))],
        out_specs=pl.BlockSpec((1, tm, d), lambda i, t: (i, t, 0)),
        out_shape=jax.ShapeDtypeStruct((b, seq, d), h.dtype),
        compiler_params=_cparams(2),
        name="final_norm",
    )(h, g.reshape(1, d))


def _lane_row(vals, offset):
    return jnp.zeros((1, LANE), F32).at[0, offset:offset + vals.shape[0]].set(vals.astype(F32))


def kernel(x, meta_tokens, a_norm, a_w_in, a_conv, a_log, a_dt_bias, a_onorm, a_w_o, kv_norm, w_kv, lambda_k1, lambda_k2, b_norm, b_w_q, b_lambda_q1, b_lambda_q2, b_subln, b_w_o, ffn_norm, ffn_w_up, ffn_conv, ffn_w_down, final_norm):
    bn, seq, d = x.shape
    n_a = a_w_in.shape[0]
    n_b = b_w_q.shape[0]
    depth = n_a + n_b
    nh = GDN_HEADS
    qkvz_w = 4 * nh * LANE
    assert d == nh * LANE and seq % LANE == 0 and (LANE + seq) % ROW_TILE == 0
    d_ff = ffn_w_down.shape[1]
    assert d_ff % FF_TILE == 0
    nf = d_ff // FF_TILE

    meta = jnp.broadcast_to(meta_tokens.astype(x.dtype)[None], (bn, N_META, d))
    h = jnp.concatenate([jnp.zeros((bn, PADF, d), x.dtype), meta, x], axis=1)

    def ffn_layer(h, layer):
        w_up = ffn_w_up[layer].astype(BF16).reshape(d, 2, nf, FF_TILE).transpose(1, 2, 0, 3)
        cw = ffn_conv[layer].astype(F32).reshape(FFN_CONV, 2, nf, FF_TILE).transpose(1, 2, 0, 3)
        w_dn = ffn_w_down[layer].astype(BF16).reshape(nf, FF_TILE, d)
        return _ffn(h, ffn_norm[layer], w_up[0], w_up[1], cw[0], cw[1], w_dn)

    for layer in range(n_a):
        w_in = a_w_in[layer]
        w_main = w_in[:, :qkvz_w].astype(BF16)
        w_gate = jnp.pad(w_in[:, qkvz_w:], ((0, 0), (0, LANE - 2 * nh))).astype(BF16)
        proj, gates = _norm_matmul(h, a_norm[layer], [w_main, w_gate], [BF16, F32], [1.0, 1.0])
        o = _gdn_core(proj, gates, a_conv[layer].astype(F32), _lane_row(a_log[layer], nh),
                      _lane_row(a_dt_bias[layer], nh), a_onorm[layer].astype(F32))
        h = _res_matmul(h, o, a_w_o[layer].astype(BF16))
        h = ffn_layer(h, layer)

    diff_w = DIFF_HEADS * LANE
    k_sh, vt_sh = _norm_matmul(h, kv_norm, [w_kv[:, :diff_w].astype(BF16), w_kv[:, diff_w:].astype(BF16)],
                               [BF16, BF16], [1.0, 1.0], transposed=[False, True])
    slopes = jnp.exp2(-(8.0 / DIFF_HEADS) * jnp.arange(1, DIFF_HEADS + 1, dtype=F32))
    slopes = jnp.broadcast_to(slopes[:, None, None], (DIFF_HEADS, 1, LANE))
    dh = lambda_k1.shape[0]
    for j in range(n_b):
        layer = n_a + j
        lam_init = 0.8 - 0.6 * math.exp(-0.3 * layer)
        (q,) = _norm_matmul(h, b_norm[j], [b_w_q[j].astype(BF16)], [BF16], [dh ** -0.5])
        o = _diff_attn(q, k_sh, vt_sh, slopes, b_lambda_q1[j], lambda_k1, b_lambda_q2[j], lambda_k2,
                       b_subln[j], lam_init)
        h = _res_matmul(h, o, b_w_o[j].astype(BF16))
        h = ffn_layer(h, layer)

    return _final_norm(h, final_norm, seq)
```

```python
import functools
import math

import jax
import jax.numpy as jnp
from jax import lax
from jax.experimental import pallas as pl
from jax.experimental.pallas import tpu as pltpu

N_META = 16
GDN_HEADS = 8
GDN_CONV = 4
DIFF_HEADS = 8
FFN_CONV = 3
EPS = 1e-6

LANE = 128
PADF = LANE - N_META
CHUNK = 128
ROW_TILE = 384
KV_TILE = 256
Q_CHUNK = 256
FF_TILE = 256
NEG = -1e30
VMEM_LIMIT = 56 * 1024 * 1024

F32 = jnp.float32
BF16 = jnp.bfloat16


def _cparams(n_axes):
    return pltpu.CompilerParams(
        dimension_semantics=("arbitrary",) * n_axes,
        vmem_limit_bytes=VMEM_LIMIT,
    )


def _const_spec(shape):
    zeros = (0,) * len(shape)
    return pl.BlockSpec(shape, lambda *_: zeros, pipeline_mode=pl.Buffered(1))


def _rms(x, g):
    return x * lax.rsqrt(jnp.mean(x * x, axis=-1, keepdims=True) + EPS) * g


def _keep_rows(t, rows):
    r = lax.broadcasted_iota(jnp.int32, (rows, 1), 0)
    return jnp.logical_or(t > 0, r >= PADF)


def _dot(a, b):
    return jnp.dot(a, b, preferred_element_type=F32)


def _dot_nt(a, b):
    return lax.dot_general(a, b, (((1,), (1,)), ((), ())), preferred_element_type=F32)


def _dot_tn(a, b):
    return lax.dot_general(a, b, (((0,), (0,)), ((), ())), preferred_element_type=F32)


def _norm_matmul_kernel(*refs, n_out, scales, transposed, n_chunk):
    h_ref, g_ref = refs[0], refs[1]
    w_refs = refs[2:2 + n_out]
    o_refs = refs[2 + n_out:2 + 2 * n_out]
    xn = _rms(h_ref[0], g_ref[...]).astype(BF16)
    for w_ref, o_ref, scale, tr in zip(w_refs, o_refs, scales, transposed):
        n = w_ref.shape[-1]
        step = min(n_chunk, n)
        for n0 in range(0, n, step):
            acc = _dot(xn, w_ref[:, n0:n0 + step])
            if scale != 1.0:
                acc = acc * scale
            if tr:
                o_ref[0, n0:n0 + step, :] = acc.T.astype(o_ref.dtype)
            else:
                o_ref[0, :, n0:n0 + step] = acc.astype(o_ref.dtype)


def _norm_matmul(h, g, ws, out_dtypes, scales, transposed=None):
    b, lp, d = h.shape
    tm = ROW_TILE
    transposed = tuple(transposed) if transposed is not None else (False,) * len(ws)
    kern = functools.partial(_norm_matmul_kernel, n_out=len(ws), scales=tuple(scales),
                             transposed=transposed, n_chunk=512)
    out_specs, out_shape = [], []
    for w, dt, tr in zip(ws, out_dtypes, transposed):
        n = w.shape[1]
        if tr:
            out_specs.append(pl.BlockSpec((1, n, tm), lambda i, t: (i, 0, t)))
            out_shape.append(jax.ShapeDtypeStruct((b, n, lp), dt))
        else:
            out_specs.append(pl.BlockSpec((1, tm, n), lambda i, t: (i, t, 0)))
            out_shape.append(jax.ShapeDtypeStruct((b, lp, n), dt))
    return pl.pallas_call(
        kern,
        grid=(b, lp // tm),
        in_specs=[pl.BlockSpec((1, tm, d), lambda i, t: (i, t, 0)), _const_spec((1, d))]
        + [_const_spec(w.shape) for w in ws],
        out_specs=out_specs,
        out_shape=out_shape,
        compiler_params=_cparams(2),
        name="norm_matmul",
    )(h, g.reshape(1, d), *ws)


def _res_matmul_kernel(h_ref, y_ref, w_ref, o_ref):
    t = pl.program_id(1)
    out = h_ref[0] + _dot(y_ref[0], w_ref[...])
    o_ref[0] = jnp.where(_keep_rows(t, out.shape[0]), out, 0.0)


def _res_matmul(h, y, w):
    b, lp, d = h.shape
    tm = ROW_TILE
    return pl.pallas_call(
        _res_matmul_kernel,
        grid=(b, lp // tm),
        in_specs=[pl.BlockSpec((1, tm, d), lambda i, t: (i, t, 0)),
                  pl.BlockSpec((1, tm, y.shape[2]), lambda i, t: (i, t, 0)),
                  _const_spec(w.shape)],
        out_specs=pl.BlockSpec((1, tm, d), lambda i, t: (i, t, 0)),
        out_shape=jax.ShapeDtypeStruct(h.shape, h.dtype),
        input_output_aliases={0: 0},
        compiler_params=_cparams(2),
        name="res_matmul",
    )(h, y, w)


def _bdot(a, b):
    return _dot(a.astype(BF16), b.astype(BF16))


def _tri_inverse(mats, row, col):
    n = mats[0].shape[0]
    eye = (row == col).astype(F32)

    def same_block(s):
        sh = s.bit_length() - 1
        return lax.shift_right_logical(row, sh) == lax.shift_right_logical(col, sh)

    nb = [jnp.where(same_block(8), -a, 0.0).astype(BF16) for a in mats]
    n2 = [_dot(x, x) for x in nb]
    p = [eye + x.astype(F32) for x in nb]
    p = [x + _bdot(x, y) for x, y in zip(p, n2)]
    n4 = [_bdot(y, y) for y in n2]
    p = [x + _bdot(x, y) for x, y in zip(p, n4)]
    s = 8
    while s < n:
        pick = jnp.logical_and(same_block(2 * s), jnp.logical_not(same_block(s)))
        off = [jnp.where(pick, a, 0.0).astype(BF16) for a in mats]
        pb = [x.astype(BF16) for x in p]
        y = [_dot(o, x) for o, x in zip(off, pb)]
        p = [x - _bdot(xb, yy) for x, xb, yy in zip(p, pb, y)]
        s *= 2
    return p


def _gdn_kernel(x_ref, gt_ref, cw_ref, alog_ref, dtb_ref, og_ref, o_ref, state_ref, carry_ref):
    t = pl.program_id(1)
    c = CHUNK
    nh = GDN_HEADS
    dk = LANE
    qk_w = nh * dk

    @pl.when(t == 0)
    def _():
        state_ref[...] = jnp.zeros_like(state_ref)
        carry_ref[...] = jnp.zeros_like(carry_ref)

    x = x_ref[0, :, 0:3 * qk_w].astype(F32)
    ext = jnp.concatenate([carry_ref[...], x], axis=0)
    cw = cw_ref[...]
    conv = (cw[3:4] * x + cw[2:3] * ext[7:7 + c] + cw[1:2] * ext[6:6 + c] + cw[0:1] * ext[5:5 + c])
    carry_ref[...] = x[c - 8:c]
    qkv = conv * jax.nn.sigmoid(conv)

    keep = _keep_rows(t, c)
    gts = gt_ref[0]
    beta_all = jnp.where(keep, jax.nn.sigmoid(gts), 0.0)
    g_all = -jnp.exp(alog_ref[...]) * jax.nn.softplus(gts + dtb_ref[...])
    g_all = jnp.where(keep, g_all, 0.0)

    row = lax.broadcasted_iota(jnp.int32, (c, c), 0)
    col = lax.broadcasted_iota(jnp.int32, (c, c), 1)
    causal = col <= row
    strict = col < row
    tril = causal.astype(F32)
    triu = (row <= col).astype(F32)
    gc_all = jnp.dot(tril, g_all, preferred_element_type=F32, precision=lax.Precision.HIGHEST)
    gct_all = jnp.dot(g_all.T, triu, preferred_element_type=F32, precision=lax.Precision.HIGHEST)

    heads = range(nh)
    qn, kn, vb, kb, decay, egc, gcol = [], [], [], [], [], [], []
    for h in heads:
        qh = qkv[:, h * dk:(h + 1) * dk]
        kh = qkv[:, qk_w + h * dk:qk_w + (h + 1) * dk]
        vh = qkv[:, 2 * qk_w + h * dk:2 * qk_w + (h + 1) * dk]
        qn.append(qh * lax.rsqrt(jnp.sum(qh * qh, axis=-1, keepdims=True) + EPS) * (dk ** -0.5))
        kn.append(kh * lax.rsqrt(jnp.sum(kh * kh, axis=-1, keepdims=True) + EPS))
        beta = beta_all[:, h:h + 1]
        gcol.append(gc_all[:, nh + h:nh + h + 1])
        grow = gct_all[nh + h:nh + h + 1, :]
        decay.append(jnp.where(causal, jnp.exp(gcol[h] - grow), 0.0))
        egc.append(jnp.exp(gcol[h]))
        kb.append(kn[h] * beta)
        vb.append(vh * beta)
    kq = [_dot_nt(jnp.concatenate([kb[h], qn[h]], axis=0).astype(BF16), kn[h].astype(BF16))
          for h in heads]
    tinv = _tri_inverse([jnp.where(strict, kq[h][:c] * decay[h], 0.0) for h in heads], row, col)
    uw = [_bdot(tinv[h], jnp.concatenate([vb[h], kb[h] * egc[h]], axis=1)) for h in heads]
    state = [state_ref[h] for h in heads]
    ws_qs = [_bdot(jnp.concatenate([uw[h][:, dk:], qn[h] * egc[h]], axis=0), state[h]) for h in heads]
    v_new = [uw[h][:, :dk] - ws_qs[h][:c] for h in heads]
    o = [ws_qs[h][c:] + _bdot(kq[h][c:] * decay[h], v_new[h]) for h in heads]
    g_last = [gcol[h][c - 1:c] for h in heads]
    kv = [_dot_tn((kn[h] * jnp.exp(g_last[h] - gcol[h])).astype(BF16), v_new[h].astype(BF16)) for h in heads]
    for h in heads:
        state_ref[h] = state[h] * jnp.exp(g_last[h]) + kv[h]
        z = x_ref[0, :, 3 * qk_w + h * dk:3 * qk_w + (h + 1) * dk].astype(F32)
        on = _rms(o[h], og_ref[...])
        o_ref[0, :, h * dk:(h + 1) * dk] = (on * (z * jax.nn.sigmoid(z))).astype(o_ref.dtype)


def _gdn_core(proj, gates, conv_w, alog_row, dtb_row, o_gain):
    b, lp, width = proj.shape
    v_w = GDN_HEADS * LANE
    c = CHUNK
    return pl.pallas_call(
        _gdn_kernel,
        grid=(b, lp // c),
        in_specs=[pl.BlockSpec((1, c, width), lambda i, t: (i, t, 0)),
                  pl.BlockSpec((1, c, LANE), lambda i, t: (i, t, 0)),
                  _const_spec(conv_w.shape), _const_spec((1, LANE)), _const_spec((1, LANE)),
                  _const_spec((1, LANE))],
        out_specs=pl.BlockSpec((1, c, v_w), lambda i, t: (i, t, 0)),
        out_shape=jax.ShapeDtypeStruct((b, lp, v_w), BF16),
        scratch_shapes=[pltpu.VMEM((GDN_HEADS, LANE, LANE), F32),
                        pltpu.VMEM((8, 3 * v_w), F32)],
        compiler_params=_cparams(2),
        name="gdn_core",
    )(proj, gates, conv_w, alog_row, dtb_row, o_gain.reshape(1, LANE))


def _diff_attn_kernel(q_ref, k_ref, vt_ref, slope_ref, lq1_ref, lk1_ref, lq2_ref, lk2_ref, sub_ref,
                      o_ref, qq_ref, bias_ref, *chunk_refs, lam_init):
    i = pl.program_id(2)
    tq = q_ref.shape[1]
    tk = KV_TILE
    lp = k_ref.shape[1]
    dh = LANE // 2
    qc = Q_CHUNK
    n_chunks = 2 * tq // qc
    s_refs = chunk_refs[0::4]
    m_refs = chunk_refs[1::4]
    l_refs = chunk_refs[2::4]
    acc_refs = chunk_refs[3::4]

    q = q_ref[0]
    lane = lax.broadcasted_iota(jnp.int32, (1, LANE), 1)
    zero = jnp.zeros_like(q)
    qq_ref[0:tq, :] = jnp.where(lane < dh, q, zero)
    qq_ref[tq:2 * tq, :] = jnp.where(lane >= dh, q, zero)

    slope = slope_ref[0][:, 0:1]
    krel = lax.broadcasted_iota(jnp.int32, (tk, qc), 0)
    bias_ref[...] = slope * krel.astype(F32)
    c2 = lax.broadcasted_iota(jnp.int32, (1, 2 * tq), 1)
    qpos = i * tq + jnp.where(c2 >= tq, c2 - tq, c2)

    def block_start(j):
        return pl.multiple_of(jnp.minimum(j * tk, lp - tk), LANE)

    def scores(k, ci):
        return _dot_nt(k, qq_ref[ci * qc:(ci + 1) * qc, :])

    k0 = k_ref[0, pl.ds(block_start(0), tk), :]
    for ci in range(n_chunks):
        s_refs[ci][...] = scores(k0, ci)
        m_refs[ci][...] = jnp.full_like(m_refs[ci], NEG)
        l_refs[ci][...] = jnp.zeros_like(l_refs[ci])
        acc_refs[ci][...] = jnp.zeros_like(acc_refs[ci])

    def step(j, masked):
        start = block_start(j)
        k_next = k_ref[0, pl.ds(block_start(j + 1), tk), :]
        vt = vt_ref[0, :, pl.ds(start, tk)]
        c = slope * jnp.asarray(start, F32)
        bias = bias_ref[...]
        for ci in range(n_chunks):
            s = s_refs[ci][...] + bias
            s_next = scores(k_next, ci)
            if masked:
                kpos = start + krel
                valid = jnp.logical_and(kpos <= qpos[:, ci * qc:(ci + 1) * qc],
                                        kpos >= jnp.maximum(j * tk, PADF))
                s = jnp.where(valid, s, NEG)
            m_prev = m_refs[ci][...]
            m_new = jnp.maximum(m_prev, jnp.max(s, axis=0, keepdims=True) + c)
            alpha = jnp.exp(m_prev - m_new)
            p = jnp.exp(s - (m_new - c))
            l_refs[ci][...] = alpha * l_refs[ci][...] + jnp.sum(p, axis=0, keepdims=True)
            acc_refs[ci][...] = alpha * acc_refs[ci][...] + _dot(vt, p.astype(BF16))
            m_refs[ci][...] = m_new
            s_refs[ci][...] = s_next

    def masked_body(j, carry):
        step(j, True)
        return carry

    def plain_body(j, carry):
        step(j, False)
        return carry

    n_kv = ((i + 1) * tq + tk - 1) // tk
    n_plain_end = jnp.maximum((i * tq + 1) // tk, 1)
    step(0, True)
    lax.fori_loop(1, n_plain_end, plain_body, 0)
    lax.fori_loop(n_plain_end, n_kv, masked_body, 0)

    o = jnp.concatenate([acc_refs[ci][...] / l_refs[ci][...] for ci in range(n_chunks)], axis=1)
    lam = (jnp.exp(jnp.sum(lq1_ref[...] * lk1_ref[...], axis=-1, keepdims=True))
           - jnp.exp(jnp.sum(lq2_ref[...] * lk2_ref[...], axis=-1, keepdims=True)) + lam_init)
    od = o[:, :tq] - lam * o[:, tq:]
    on = od * lax.rsqrt(jnp.mean(od * od, axis=0, keepdims=True) + EPS) * sub_ref[...]
    o_ref[0] = (on * (1.0 - lam_init)).T.astype(o_ref.dtype)


def _diff_attn(q, k, vt, slopes, lq1, lk1, lq2, lk2, subln, lam_init):
    b, lp, width = q.shape
    nh = DIFF_HEADS
    tq = ROW_TILE
    dh = lq1.shape[-1]
    kern = functools.partial(_diff_attn_kernel, lam_init=lam_init)
    vec = lambda a: a.reshape(1, dh).astype(F32)
    chunk_scratch = [pltpu.VMEM((KV_TILE, Q_CHUNK), F32), pltpu.VMEM((1, Q_CHUNK), F32),
                     pltpu.VMEM((1, Q_CHUNK), F32), pltpu.VMEM((LANE, Q_CHUNK), F32)]
    return pl.pallas_call(
        kern,
        grid=(b, nh, lp // tq),
        in_specs=[pl.BlockSpec((1, tq, LANE), lambda bi, h, i: (bi, i, h)),
                  pl.BlockSpec((1, lp, LANE), lambda bi, h, i: (bi, 0, h)),
                  pl.BlockSpec((1, LANE, lp), lambda bi, h, i: (bi, h, 0)),
                  pl.BlockSpec((1, 1, LANE), lambda bi, h, i: (h, 0, 0)),
                  _const_spec((1, dh)), _const_spec((1, dh)), _const_spec((1, dh)), _const_spec((1, dh)),
                  _const_spec((LANE, 1))],
        out_specs=pl.BlockSpec((1, tq, LANE), lambda bi, h, i: (bi, i, h)),
        out_shape=jax.ShapeDtypeStruct((b, lp, width), BF16),
        scratch_shapes=[pltpu.VMEM((2 * tq, LANE), BF16), pltpu.VMEM((KV_TILE, Q_CHUNK), F32)]
        + chunk_scratch * (2 * tq // Q_CHUNK),
        compiler_params=_cparams(3),
        name="diff_attn",
    )(q, k, vt, slopes, vec(lq1), vec(lk1), vec(lq2), vec(lk2), subln.reshape(LANE, 1).astype(F32))


def _ffn_kernel(h_ref, g_ref, wg_ref, wu_ref, cg_ref, cu_ref, wd_ref, o_ref,
                xn_ref, acc_ref, carry_g_ref, carry_u_ref, ug_a, uu_a, ug_b, uu_b):
    t = pl.program_id(1)
    tm = h_ref.shape[1]
    nf = wg_ref.shape[0]
    slot_a = (ug_a, uu_a)
    slot_b = (ug_b, uu_b)

    @pl.when(t == 0)
    def _():
        carry_g_ref[...] = jnp.zeros_like(carry_g_ref)
        carry_u_ref[...] = jnp.zeros_like(carry_u_ref)

    x = h_ref[0]
    xn_ref[...] = _rms(x, g_ref[...]).astype(BF16)
    acc_ref[...] = jnp.zeros_like(acc_ref)

    rc = LANE

    def up_gate(f, slot):
        slot[0][8:8 + tm, :] = _dot(xn_ref[...], wg_ref[f])

    def up_up(f, slot):
        slot[1][8:8 + tm, :] = _dot(xn_ref[...], wu_ref[f])

    def load_halo(f, slot):
        for u_ref, carry_ref in ((slot[0], carry_g_ref), (slot[1], carry_u_ref)):
            u_ref[0:8, :] = carry_ref[f]
            carry_ref[f] = u_ref[tm:tm + 8, :]

    def conv(u_ref, cw, r0):
        return (cw[2:3] * u_ref[8 + r0:8 + r0 + rc, :] + cw[1:2] * u_ref[7 + r0:7 + r0 + rc, :]
                + cw[0:1] * u_ref[6 + r0:6 + r0 + rc, :])

    def act_down(f, slot, r0):
        gate = conv(slot[0], cg_ref[f], r0)
        up = conv(slot[1], cu_ref[f], r0)
        act = (gate * jax.nn.sigmoid(gate) * up).astype(BF16)
        acc_ref[r0:r0 + rc, :] += _dot(act, wd_ref[f])

    def stage(f, cur, other, last=False):
        load_halo(f, cur)
        if not last:
            up_gate(f + 1, other)
        act_down(f, cur, 0)
        if not last:
            up_up(f + 1, other)
        for r0 in range(rc, tm, rc):
            act_down(f, cur, r0)

    def pair(p, carry):
        f = 2 * p
        stage(f, slot_a, slot_b)
        stage(f + 1, slot_b, slot_a)
        return carry

    up_gate(0, slot_a)
    up_up(0, slot_a)
    lax.fori_loop(0, (nf - 1) // 2, pair, 0)
    stage(nf - 1, slot_a, slot_b, last=True)
    out = x + acc_ref[...]
    o_ref[0] = jnp.where(_keep_rows(t, tm), out, 0.0)


def _ffn(h, g, wg, wu, cg, cu, wd):
    b, lp, d = h.shape
    tm = ROW_TILE
    nf, _, tf = wg.shape
    return pl.pallas_call(
        _ffn_kernel,
        grid=(b, lp // tm),
        in_specs=[pl.BlockSpec((1, tm, d), lambda i, t: (i, t, 0)), _const_spec((1, d)),
                  _const_spec(wg.shape), _const_spec(wu.shape), _const_spec(cg.shape),
                  _const_spec(cu.shape), _const_spec(wd.shape)],
        out_specs=pl.BlockSpec((1, tm, d), lambda i, t: (i, t, 0)),
        out_shape=jax.ShapeDtypeStruct(h.shape, h.dtype),
        scratch_shapes=[pltpu.VMEM((tm, d), BF16), pltpu.VMEM((tm, d), F32),
                        pltpu.VMEM((nf, 8, tf), F32), pltpu.VMEM((nf, 8, tf), F32)]
        + [pltpu.VMEM((tm + 8, tf), F32)] * 4,
        input_output_aliases={0: 0},
        compiler_params=_cparams(2),
        name="conv_ffn",
    )(h, g.reshape(1, d), wg, wu, cg, cu, wd)


def _final_norm_kernel(h_ref, g_ref, o_ref):
    o_ref[0] = _rms(h_ref[0], g_ref[...])


def _final_norm(h, g, seq):
    b, lp, d = h.shape
    tm = LANE
    return pl.pallas_call(
        _final_norm_kernel,
        grid=(b, seq // tm),
        in_specs=[pl.BlockSpec((1, tm, d), lambda i, t: (i, t + 1, 0)), _const_spec((1, d))],
        out_specs=pl.BlockSpec((1, tm, d), lambda i, t: (i, t, 0)),
        out_shape=jax.ShapeDtypeStruct((b, seq, d), h.dtype),
        compiler_params=_cparams(2),
        name="final_norm",
    )(h, g.reshape(1, d))


def _lane_row(vals, offset):
    return jnp.zeros((1, LANE), F32).at[0, offset:offset + vals.shape[0]].set(vals.astype(F32))


def kernel(x, meta_tokens, a_norm, a_w_in, a_conv, a_log, a_dt_bias, a_onorm, a_w_o, kv_norm, w_kv, lambda_k1, lambda_k2, b_norm, b_w_q, b_lambda_q1, b_lambda_q2, b_subln, b_w_o, ffn_norm, ffn_w_up, ffn_conv, ffn_w_down, final_norm):
    bn, seq, d = x.shape
    n_a = a_w_in.shape[0]
    n_b = b_w_q.shape[0]
    depth = n_a + n_b
    nh = GDN_HEADS
    qkvz_w = 4 * nh * LANE
    assert d == nh * LANE and seq % LANE == 0 and (LANE + seq) % ROW_TILE == 0
    d_ff = ffn_w_down.shape[1]
    assert d_ff % FF_TILE == 0
    nf = d_ff // FF_TILE
    assert nf >= 3 and nf % 2 == 1

    meta = jnp.broadcast_to(meta_tokens.astype(x.dtype)[None], (bn, N_META, d))
    h = jnp.concatenate([jnp.zeros((bn, PADF, d), x.dtype), meta, x], axis=1)

    def ffn_layer(h, layer):
        w_up = ffn_w_up[layer].astype(BF16).reshape(d, 2, nf, FF_TILE).transpose(1, 2, 0, 3)
        cw = ffn_conv[layer].astype(F32).reshape(FFN_CONV, 2, nf, FF_TILE).transpose(1, 2, 0, 3)
        w_dn = ffn_w_down[layer].astype(BF16).reshape(nf, FF_TILE, d)
        return _ffn(h, ffn_norm[layer], w_up[0], w_up[1], cw[0], cw[1], w_dn)

    for layer in range(n_a):
        w_in = a_w_in[layer]
        w_main = w_in[:, :qkvz_w].astype(BF16)
        w_gate = jnp.pad(w_in[:, qkvz_w:], ((0, 0), (0, LANE - 2 * nh))).astype(BF16)
        proj, gates = _norm_matmul(h, a_norm[layer], [w_main, w_gate], [BF16, F32], [1.0, 1.0])
        o = _gdn_core(proj, gates, a_conv[layer].astype(F32), _lane_row(a_log[layer], nh),
                      _lane_row(a_dt_bias[layer], nh), a_onorm[layer].astype(F32))
        h = _res_matmul(h, o, a_w_o[layer].astype(BF16))
        h = ffn_layer(h, layer)

    diff_w = DIFF_HEADS * LANE
    k_sh, vt_sh = _norm_matmul(h, kv_norm, [w_kv[:, :diff_w].astype(BF16), w_kv[:, diff_w:].astype(BF16)],
                               [BF16, BF16], [1.0, 1.0], transposed=[False, True])
    slopes = jnp.exp2(-(8.0 / DIFF_HEADS) * jnp.arange(1, DIFF_HEADS + 1, dtype=F32))
    slopes = jnp.broadcast_to(slopes[:, None, None], (DIFF_HEADS, 1, LANE))
    dh = lambda_k1.shape[0]
    for j in range(n_b):
        layer = n_a + j
        lam_init = 0.8 - 0.6 * math.exp(-0.3 * layer)
        (q,) = _norm_matmul(h, b_norm[j], [b_w_q[j].astype(BF16)], [BF16], [dh ** -0.5])
        o = _diff_attn(q, k_sh, vt_sh, slopes, b_lambda_q1[j], lambda_k1, b_lambda_q2[j], lambda_k2,
                       b_subln[j], lam_init)
        h = _res_matmul(h, o, b_w_o[j].astype(BF16))
        h = ffn_layer(h, layer)

    return _final_norm(h, final_norm, seq)
```

```python
import functools
import math

import jax
import jax.numpy as jnp
from jax import lax
from jax.experimental import pallas as pl
from jax.experimental.pallas import tpu as pltpu

N_META = 16
GDN_HEADS = 8
GDN_CONV = 4
DIFF_HEADS = 8
FFN_CONV = 3
EPS = 1e-6

LANE = 128
PADF = LANE - N_META
CHUNK = 128
ROW_TILE = 384
KV_TILE = 256
Q_CHUNK = 256
FF_TILE = 256
NEG = -1e30
VMEM_LIMIT = 56 * 1024 * 1024

F32 = jnp.float32
BF16 = jnp.bfloat16


def _cparams(n_axes):
    return pltpu.CompilerParams(
        dimension_semantics=("arbitrary",) * n_axes,
        vmem_limit_bytes=VMEM_LIMIT,
    )


def _const_spec(shape):
    zeros = (0,) * len(shape)
    return pl.BlockSpec(shape, lambda *_: zeros, pipeline_mode=pl.Buffered(1))


def _rms(x, g):
    return x * lax.rsqrt(jnp.mean(x * x, axis=-1, keepdims=True) + EPS) * g


def _keep_rows(t, rows):
    r = lax.broadcasted_iota(jnp.int32, (rows, 1), 0)
    return jnp.logical_or(t > 0, r >= PADF)


def _dot(a, b):
    return jnp.dot(a, b, preferred_element_type=F32)


def _dot_nt(a, b):
    return lax.dot_general(a, b, (((1,), (1,)), ((), ())), preferred_element_type=F32)


def _dot_tn(a, b):
    return lax.dot_general(a, b, (((0,), (0,)), ((), ())), preferred_element_type=F32)


def _norm_matmul_kernel(*refs, n_out, scales, transposed, n_chunk):
    h_ref, g_ref = refs[0], refs[1]
    w_refs = refs[2:2 + n_out]
    o_refs = refs[2 + n_out:2 + 2 * n_out]
    xn = _rms(h_ref[0], g_ref[...]).astype(BF16)
    for w_ref, o_ref, scale, tr in zip(w_refs, o_refs, scales, transposed):
        n = w_ref.shape[-1]
        step = min(n_chunk, n)
        for n0 in range(0, n, step):
            acc = _dot(xn, w_ref[:, n0:n0 + step])
            if scale != 1.0:
                acc = acc * scale
            if tr:
                o_ref[0, n0:n0 + step, :] = acc.T.astype(o_ref.dtype)
            else:
                o_ref[0, :, n0:n0 + step] = acc.astype(o_ref.dtype)


def _norm_matmul(h, g, ws, out_dtypes, scales, transposed=None):
    b, lp, d = h.shape
    tm = ROW_TILE
    transposed = tuple(transposed) if transposed is not None else (False,) * len(ws)
    kern = functools.partial(_norm_matmul_kernel, n_out=len(ws), scales=tuple(scales),
                             transposed=transposed, n_chunk=512)
    out_specs, out_shape = [], []
    for w, dt, tr in zip(ws, out_dtypes, transposed):
        n = w.shape[1]
        if tr:
            out_specs.append(pl.BlockSpec((1, n, tm), lambda i, t: (i, 0, t)))
            out_shape.append(jax.ShapeDtypeStruct((b, n, lp), dt))
        else:
            out_specs.append(pl.BlockSpec((1, tm, n), lambda i, t: (i, t, 0)))
            out_shape.append(jax.ShapeDtypeStruct((b, lp, n), dt))
    return pl.pallas_call(
        kern,
        grid=(b, lp // tm),
        in_specs=[pl.BlockSpec((1, tm, d), lambda i, t: (i, t, 0)), _const_spec((1, d))]
        + [_const_spec(w.shape) for w in ws],
        out_specs=out_specs,
        out_shape=out_shape,
        compiler_params=_cparams(2),
        name="norm_matmul",
    )(h, g.reshape(1, d), *ws)


def _bdot(a, b):
    return _dot(a.astype(BF16), b.astype(BF16))


def _tri_inverse(mats, row, col):
    n = mats[0].shape[0]
    eye = (row == col).astype(F32)

    def same_block(s):
        sh = s.bit_length() - 1
        return lax.shift_right_logical(row, sh) == lax.shift_right_logical(col, sh)

    nb = [jnp.where(same_block(8), -a, 0.0).astype(BF16) for a in mats]
    n2 = [_dot(x, x) for x in nb]
    p = [eye + x.astype(F32) for x in nb]
    p = [x + _bdot(x, y) for x, y in zip(p, n2)]
    n4 = [_bdot(y, y) for y in n2]
    p = [x + _bdot(x, y) for x, y in zip(p, n4)]
    s = 8
    while s < n:
        pick = jnp.logical_and(same_block(2 * s), jnp.logical_not(same_block(s)))
        off = [jnp.where(pick, a, 0.0).astype(BF16) for a in mats]
        pb = [x.astype(BF16) for x in p]
        y = [_dot(o, x) for o, x in zip(off, pb)]
        p = [x - _bdot(xb, yy) for x, xb, yy in zip(p, pb, y)]
        s *= 2
    return p


def _gdn_kernel(x_ref, gt_ref, cw_ref, alog_ref, dtb_ref, og_ref, o_ref, state_ref, carry_ref):
    t = pl.program_id(1)
    c = CHUNK
    nh = GDN_HEADS
    dk = LANE
    qk_w = nh * dk

    @pl.when(t == 0)
    def _():
        state_ref[...] = jnp.zeros_like(state_ref)
        carry_ref[...] = jnp.zeros_like(carry_ref)

    x = x_ref[0, :, 0:3 * qk_w].astype(F32)
    ext = jnp.concatenate([carry_ref[...], x], axis=0)
    cw = cw_ref[...]
    conv = (cw[3:4] * x + cw[2:3] * ext[7:7 + c] + cw[1:2] * ext[6:6 + c] + cw[0:1] * ext[5:5 + c])
    carry_ref[...] = x[c - 8:c]
    qkv = conv * jax.nn.sigmoid(conv)

    keep = _keep_rows(t, c)
    gts = gt_ref[0]
    beta_all = jnp.where(keep, jax.nn.sigmoid(gts), 0.0)
    g_all = -jnp.exp(alog_ref[...]) * jax.nn.softplus(gts + dtb_ref[...])
    g_all = jnp.where(keep, g_all, 0.0)

    row = lax.broadcasted_iota(jnp.int32, (c, c), 0)
    col = lax.broadcasted_iota(jnp.int32, (c, c), 1)
    causal = col <= row
    strict = col < row
    tril = causal.astype(F32)
    triu = (row <= col).astype(F32)
    gc_all = jnp.dot(tril, g_all, preferred_element_type=F32, precision=lax.Precision.HIGHEST)
    gct_all = jnp.dot(g_all.T, triu, preferred_element_type=F32, precision=lax.Precision.HIGHEST)

    heads = range(nh)
    qn, kn, vb, kb, decay, egc, gcol = [], [], [], [], [], [], []
    for h in heads:
        qh = qkv[:, h * dk:(h + 1) * dk]
        kh = qkv[:, qk_w + h * dk:qk_w + (h + 1) * dk]
        vh = qkv[:, 2 * qk_w + h * dk:2 * qk_w + (h + 1) * dk]
        qn.append(qh * lax.rsqrt(jnp.sum(qh * qh, axis=-1, keepdims=True) + EPS) * (dk ** -0.5))
        kn.append(kh * lax.rsqrt(jnp.sum(kh * kh, axis=-1, keepdims=True) + EPS))
        beta = beta_all[:, h:h + 1]
        gcol.append(gc_all[:, nh + h:nh + h + 1])
        grow = gct_all[nh + h:nh + h + 1, :]
        decay.append(jnp.where(causal, jnp.exp(gcol[h] - grow), 0.0))
        egc.append(jnp.exp(gcol[h]))
        kb.append(kn[h] * beta)
        vb.append(vh * beta)
    kq = [_dot_nt(jnp.concatenate([kb[h], qn[h]], axis=0).astype(BF16), kn[h].astype(BF16))
          for h in heads]
    tinv = _tri_inverse([jnp.where(strict, kq[h][:c] * decay[h], 0.0) for h in heads], row, col)
    uw = [_bdot(tinv[h], jnp.concatenate([vb[h], kb[h] * egc[h]], axis=1)) for h in heads]
    state = [state_ref[h] for h in heads]
    ws_qs = [_bdot(jnp.concatenate([uw[h][:, dk:], qn[h] * egc[h]], axis=0), state[h]) for h in heads]
    v_new = [uw[h][:, :dk] - ws_qs[h][:c] for h in heads]
    o = [ws_qs[h][c:] + _bdot(kq[h][c:] * decay[h], v_new[h]) for h in heads]
    g_last = [gcol[h][c - 1:c] for h in heads]
    kv = [_dot_tn((kn[h] * jnp.exp(g_last[h] - gcol[h])).astype(BF16), v_new[h].astype(BF16)) for h in heads]
    for h in heads:
        state_ref[h] = state[h] * jnp.exp(g_last[h]) + kv[h]
        z = x_ref[0, :, 3 * qk_w + h * dk:3 * qk_w + (h + 1) * dk].astype(F32)
        on = _rms(o[h], og_ref[...])
        o_ref[0, :, h * dk:(h + 1) * dk] = (on * (z * jax.nn.sigmoid(z))).astype(o_ref.dtype)


def _gdn_core(proj, gates, conv_w, alog_row, dtb_row, o_gain):
    b, lp, width = proj.shape
    v_w = GDN_HEADS * LANE
    c = CHUNK
    return pl.pallas_call(
        _gdn_kernel,
        grid=(b, lp // c),
        in_specs=[pl.BlockSpec((1, c, width), lambda i, t: (i, t, 0)),
                  pl.BlockSpec((1, c, LANE), lambda i, t: (i, t, 0)),
                  _const_spec(conv_w.shape), _const_spec((1, LANE)), _const_spec((1, LANE)),
                  _const_spec((1, LANE))],
        out_specs=pl.BlockSpec((1, c, v_w), lambda i, t: (i, t, 0)),
        out_shape=jax.ShapeDtypeStruct((b, lp, v_w), BF16),
        scratch_shapes=[pltpu.VMEM((GDN_HEADS, LANE, LANE), F32),
                        pltpu.VMEM((8, 3 * v_w), F32)],
        compiler_params=_cparams(2),
        name="gdn_core",
    )(proj, gates, conv_w, alog_row, dtb_row, o_gain.reshape(1, LANE))


def _diff_attn_kernel(q_ref, k_ref, vt_ref, slope_ref, lq1_ref, lk1_ref, lq2_ref, lk2_ref, sub_ref,
                      o_ref, qq_ref, bias_ref, *chunk_refs, lam_init):
    i = pl.program_id(2)
    tq = q_ref.shape[1]
    tk = KV_TILE
    lp = k_ref.shape[1]
    dh = LANE // 2
    qc = Q_CHUNK
    n_chunks = 2 * tq // qc
    s_refs, p_refs, a_refs, m_refs, l_refs, acc_refs = (chunk_refs[n::6] for n in range(6))

    q = q_ref[0]
    lane = lax.broadcasted_iota(jnp.int32, (1, LANE), 1)
    zero = jnp.zeros_like(q)
    qq_ref[0:tq, :] = jnp.where(lane < dh, q, zero)
    qq_ref[tq:2 * tq, :] = jnp.where(lane >= dh, q, zero)

    slope = slope_ref[0][:, 0:1]
    krel = lax.broadcasted_iota(jnp.int32, (tk, qc), 0)
    bias_ref[...] = slope * krel.astype(F32)
    c2 = lax.broadcasted_iota(jnp.int32, (1, 2 * tq), 1)
    qpos = i * tq + jnp.where(c2 >= tq, c2 - tq, c2)

    def block_start(j):
        return pl.multiple_of(jnp.minimum(j * tk, lp - tk), LANE)

    def scores(k, ci):
        return _dot_nt(k, qq_ref[ci * qc:(ci + 1) * qc, :])

    n_kv = ((i + 1) * tq + tk - 1) // tk
    n_plain = jnp.maximum((i * tq + 1) // tk, 1) - 1

    def block_at(t):
        return jnp.where(t < n_plain, t + 1, jnp.where(t == n_plain, 0, t))

    k_first = k_ref[0, pl.ds(block_start(block_at(0)), tk), :]
    for ci in range(n_chunks):
        s_refs[ci][...] = scores(k_first, ci)
        p_refs[ci][...] = jnp.zeros_like(p_refs[ci])
        a_refs[ci][...] = jnp.ones_like(a_refs[ci])
        m_refs[ci][...] = jnp.full_like(m_refs[ci], NEG)
        l_refs[ci][...] = jnp.zeros_like(l_refs[ci])
        acc_refs[ci][...] = jnp.zeros_like(acc_refs[ci])

    def accumulate(ci, vt_prev):
        acc_refs[ci][...] = a_refs[ci][...] * acc_refs[ci][...] + _dot(vt_prev, p_refs[ci][...])

    def step(t, start_prev, masked):
        j = block_at(t)
        start = block_start(j)
        k_next = k_ref[0, pl.ds(block_start(block_at(t + 1)), tk), :]
        vt_prev = vt_ref[0, :, pl.ds(pl.multiple_of(start_prev, LANE), tk)]
        c = slope * start.astype(F32)
        bias = bias_ref[...]
        for ci in range(n_chunks):
            accumulate(ci, vt_prev)
            s = s_refs[ci][...] + bias
            s_next = scores(k_next, ci)
            if masked:
                kpos = start + krel
                valid = jnp.logical_and(kpos <= qpos[:, ci * qc:(ci + 1) * qc],
                                        kpos >= jnp.maximum(j * tk, PADF))
                s = jnp.where(valid, s, NEG)
            m_prev = m_refs[ci][...]
            m_new = jnp.maximum(m_prev, jnp.max(s, axis=0, keepdims=True) + c)
            alpha = jnp.exp(m_prev - m_new)
            p = jnp.exp(s - (m_new - c))
            l_refs[ci][...] = alpha * l_refs[ci][...] + jnp.sum(p, axis=0, keepdims=True)
            p_refs[ci][...] = p.astype(BF16)
            a_refs[ci][...] = alpha
            m_refs[ci][...] = m_new
            s_refs[ci][...] = s_next
        return start

    start_prev = block_start(block_at(0))
    start_prev = lax.fori_loop(0, n_plain, lambda t, sp: step(t, sp, False), start_prev)
    start_prev = lax.fori_loop(n_plain, n_kv, lambda t, sp: step(t, sp, True), start_prev)
    vt_last = vt_ref[0, :, pl.ds(pl.multiple_of(start_prev, LANE), tk)]
    for ci in range(n_chunks):
        accumulate(ci, vt_last)

    o = jnp.concatenate([acc_refs[ci][...] / l_refs[ci][...] for ci in range(n_chunks)], axis=1)
    lam = (jnp.exp(jnp.sum(lq1_ref[...] * lk1_ref[...], axis=-1, keepdims=True))
           - jnp.exp(jnp.sum(lq2_ref[...] * lk2_ref[...], axis=-1, keepdims=True)) + lam_init)
    od = o[:, :tq] - lam * o[:, tq:]
    on = od * lax.rsqrt(jnp.mean(od * od, axis=0, keepdims=True) + EPS) * sub_ref[...]
    o_ref[0] = (on * (1.0 - lam_init)).T.astype(o_ref.dtype)


def _diff_attn(q, k, vt, slopes, lq1, lk1, lq2, lk2, subln, lam_init):
    b, lp, width = q.shape
    nh = DIFF_HEADS
    tq = ROW_TILE
    dh = lq1.shape[-1]
    kern = functools.partial(_diff_attn_kernel, lam_init=lam_init)
    vec = lambda a: a.reshape(1, dh).astype(F32)
    chunk_scratch = [pltpu.VMEM((KV_TILE, Q_CHUNK), F32), pltpu.VMEM((KV_TILE, Q_CHUNK), BF16),
                     pltpu.VMEM((1, Q_CHUNK), F32), pltpu.VMEM((1, Q_CHUNK), F32),
                     pltpu.VMEM((1, Q_CHUNK), F32), pltpu.VMEM((LANE, Q_CHUNK), F32)]
    return pl.pallas_call(
        kern,
        grid=(b, nh, lp // tq),
        in_specs=[pl.BlockSpec((1, tq, LANE), lambda bi, h, i: (bi, i, h)),
                  pl.BlockSpec((1, lp, LANE), lambda bi, h, i: (bi, 0, h)),
                  pl.BlockSpec((1, LANE, lp), lambda bi, h, i: (bi, h, 0)),
                  pl.BlockSpec((1, 1, LANE), lambda bi, h, i: (h, 0, 0)),
                  _const_spec((1, dh)), _const_spec((1, dh)), _const_spec((1, dh)), _const_spec((1, dh)),
                  _const_spec((LANE, 1))],
        out_specs=pl.BlockSpec((1, tq, LANE), lambda bi, h, i: (bi, i, h)),
        out_shape=jax.ShapeDtypeStruct((b, lp, width), BF16),
        scratch_shapes=[pltpu.VMEM((2 * tq, LANE), BF16), pltpu.VMEM((KV_TILE, Q_CHUNK), F32)]
        + chunk_scratch * (2 * tq // Q_CHUNK),
        compiler_params=_cparams(3),
        name="diff_attn",
    )(q, k, vt, slopes, vec(lq1), vec(lk1), vec(lq2), vec(lk2), subln.reshape(LANE, 1).astype(F32))


def _ffn_kernel(h_ref, y_ref, wo_ref, g_ref, wg_ref, wu_ref, cg_ref, cu_ref, wd_ref, fg_ref, o_ref,
                xn_ref, acc_ref, carry_g_ref, carry_u_ref, ug_a, uu_a, ug_b, uu_b, *, final):
    t = pl.program_id(1)
    tm = h_ref.shape[1]
    nf = wg_ref.shape[0]
    slot_a = (ug_a, uu_a)
    slot_b = (ug_b, uu_b)

    @pl.when(t == 0)
    def _():
        carry_g_ref[...] = jnp.zeros_like(carry_g_ref)
        carry_u_ref[...] = jnp.zeros_like(carry_u_ref)

    x = jnp.where(_keep_rows(t, tm), h_ref[0] + _dot(y_ref[0], wo_ref[...]), 0.0)
    xn_ref[...] = _rms(x, g_ref[...]).astype(BF16)
    acc_ref[...] = x

    rc = LANE

    def up_gate(f, slot):
        slot[0][8:8 + tm, :] = _dot(xn_ref[...], wg_ref[f])

    def up_up(f, slot):
        slot[1][8:8 + tm, :] = _dot(xn_ref[...], wu_ref[f])

    def load_halo(f, slot):
        for u_ref, carry_ref in ((slot[0], carry_g_ref), (slot[1], carry_u_ref)):
            u_ref[0:8, :] = carry_ref[f]
            carry_ref[f] = u_ref[tm:tm + 8, :]

    def conv(u_ref, cw, r0):
        return (cw[2:3] * u_ref[8 + r0:8 + r0 + rc, :] + cw[1:2] * u_ref[7 + r0:7 + r0 + rc, :]
                + cw[0:1] * u_ref[6 + r0:6 + r0 + rc, :])

    def act_down(f, slot, r0):
        gate = conv(slot[0], cg_ref[f], r0)
        up = conv(slot[1], cu_ref[f], r0)
        act = (gate * jax.nn.sigmoid(gate) * up).astype(BF16)
        acc_ref[r0:r0 + rc, :] += _dot(act, wd_ref[f])

    def stage(f, cur, other, last=False):
        load_halo(f, cur)
        if not last:
            up_gate(f + 1, other)
        act_down(f, cur, 0)
        if not last:
            up_up(f + 1, other)
        for r0 in range(rc, tm, rc):
            act_down(f, cur, r0)

    def pair(p, carry):
        f = 2 * p
        stage(f, slot_a, slot_b)
        stage(f + 1, slot_b, slot_a)
        return carry

    up_gate(0, slot_a)
    up_up(0, slot_a)
    lax.fori_loop(0, (nf - 1) // 2, pair, 0)
    stage(nf - 1, slot_a, slot_b, last=True)
    out = acc_ref[...]
    if final:
        out = _rms(out, fg_ref[...])
    o_ref[0] = jnp.where(_keep_rows(t, tm), out, 0.0)


def _ffn(h, y, wo, g, wg, wu, cg, cu, wd, final_gain=None):
    b, lp, d = h.shape
    tm = ROW_TILE
    nf, _, tf = wg.shape
    final = final_gain is not None
    fg = (final_gain if final else g).reshape(1, d)
    return pl.pallas_call(
        functools.partial(_ffn_kernel, final=final),
        grid=(b, lp // tm),
        in_specs=[pl.BlockSpec((1, tm, d), lambda i, t: (i, t, 0)),
                  pl.BlockSpec((1, tm, y.shape[2]), lambda i, t: (i, t, 0)),
                  _const_spec(wo.shape), _const_spec((1, d)),
                  _const_spec(wg.shape), _const_spec(wu.shape), _const_spec(cg.shape),
                  _const_spec(cu.shape), _const_spec(wd.shape), _const_spec((1, d))],
        out_specs=pl.BlockSpec((1, tm, d), lambda i, t: (i, t, 0)),
        out_shape=jax.ShapeDtypeStruct(h.shape, h.dtype),
        scratch_shapes=[pltpu.VMEM((tm, d), BF16), pltpu.VMEM((tm, d), F32),
                        pltpu.VMEM((nf, 8, tf), F32), pltpu.VMEM((nf, 8, tf), F32)]
        + [pltpu.VMEM((tm + 8, tf), F32)] * 4,
        input_output_aliases={0: 0},
        compiler_params=_cparams(2),
        name="conv_ffn",
    )(h, y, wo, g.reshape(1, d), wg, wu, cg, cu, wd, fg)


def _lane_row(vals, offset):
    return jnp.zeros((1, LANE), F32).at[0, offset:offset + vals.shape[0]].set(vals.astype(F32))


def kernel(x, meta_tokens, a_norm, a_w_in, a_conv, a_log, a_dt_bias, a_onorm, a_w_o, kv_norm, w_kv, lambda_k1, lambda_k2, b_norm, b_w_q, b_lambda_q1, b_lambda_q2, b_subln, b_w_o, ffn_norm, ffn_w_up, ffn_conv, ffn_w_down, final_norm):
    bn, seq, d = x.shape
    n_a = a_w_in.shape[0]
    n_b = b_w_q.shape[0]
    depth = n_a + n_b
    nh = GDN_HEADS
    qkvz_w = 4 * nh * LANE
    assert d == nh * LANE and seq % LANE == 0 and (LANE + seq) % ROW_TILE == 0
    d_ff = ffn_w_down.shape[1]
    assert d_ff % FF_TILE == 0
    nf = d_ff // FF_TILE
    assert nf >= 3 and nf % 2 == 1

    meta = jnp.broadcast_to(meta_tokens.astype(x.dtype)[None], (bn, N_META, d))
    h = jnp.concatenate([jnp.zeros((bn, PADF, d), x.dtype), meta, x], axis=1)

    def ffn_layer(h, mixer_out, w_o, layer):
        w_up = ffn_w_up[layer].astype(BF16).reshape(d, 2, nf, FF_TILE).transpose(1, 2, 0, 3)
        cw = ffn_conv[layer].astype(F32).reshape(FFN_CONV, 2, nf, FF_TILE).transpose(1, 2, 0, 3)
        w_dn = ffn_w_down[layer].astype(BF16).reshape(nf, FF_TILE, d)
        return _ffn(h, mixer_out, w_o.astype(BF16), ffn_norm[layer], w_up[0], w_up[1], cw[0], cw[1], w_dn,
                    final_gain=final_norm if layer == depth - 1 else None)

    for layer in range(n_a):
        w_in = a_w_in[layer]
        w_main = w_in[:, :qkvz_w].astype(BF16)
        w_gate = jnp.pad(w_in[:, qkvz_w:], ((0, 0), (0, LANE - 2 * nh))).astype(BF16)
        proj, gates = _norm_matmul(h, a_norm[layer], [w_main, w_gate], [BF16, F32], [1.0, 1.0])
        o = _gdn_core(proj, gates, a_conv[layer].astype(F32), _lane_row(a_log[layer], nh),
                      _lane_row(a_dt_bias[layer], nh), a_onorm[layer].astype(F32))
        h = ffn_layer(h, o, a_w_o[layer], layer)

    diff_w = DIFF_HEADS * LANE
    k_sh, vt_sh = _norm_matmul(h, kv_norm, [w_kv[:, :diff_w].astype(BF16), w_kv[:, diff_w:].astype(BF16)],
                               [BF16, BF16], [1.0, 1.0], transposed=[False, True])
    slopes = jnp.exp2(-(8.0 / DIFF_HEADS) * jnp.arange(1, DIFF_HEADS + 1, dtype=F32))
    slopes = jnp.broadcast_to(slopes[:, None, None], (DIFF_HEADS, 1, LANE))
    dh = lambda_k1.shape[0]
    for j in range(n_b):
        layer = n_a + j
        lam_init = 0.8 - 0.6 * math.exp(-0.3 * layer)
        (q,) = _norm_matmul(h, b_norm[j], [b_w_q[j].astype(BF16)], [BF16], [dh ** -0.5])
        o = _diff_attn(q, k_sh, vt_sh, slopes, b_lambda_q1[j], lambda_k1, b_lambda_q2[j], lambda_k2,
                       b_subln[j], lam_init)
        h = ffn_layer(h, o, b_w_o[j], layer)

    return h[:, LANE:]
```

```python
import functools
import math

import jax
import jax.numpy as jnp
from jax import lax
from jax.experimental import pallas as pl
from jax.experimental.pallas import tpu as pltpu

N_META = 16
GDN_HEADS = 8
GDN_CONV = 4
DIFF_HEADS = 8
FFN_CONV = 3
EPS = 1e-6

LANE = 128
PADF = LANE - N_META
CHUNK = 128
ROW_TILE = 384
KV_TILE = 512
Q_CHUNK = 256
FF_TILE = 256
SUM_ROWS = 16
LOG2E = math.log2(math.e)
NEG = -1e30
VMEM_LIMIT = 56 * 1024 * 1024

F32 = jnp.float32
BF16 = jnp.bfloat16


def _cparams(n_axes):
    return pltpu.CompilerParams(
        dimension_semantics=("arbitrary",) * n_axes,
        vmem_limit_bytes=VMEM_LIMIT,
    )


def _const_spec(shape):
    zeros = (0,) * len(shape)
    return pl.BlockSpec(shape, lambda *_: zeros, pipeline_mode=pl.Buffered(1))


def _rms(x, g):
    return x * lax.rsqrt(jnp.mean(x * x, axis=-1, keepdims=True) + EPS) * g


def _keep_rows(t, rows):
    r = lax.broadcasted_iota(jnp.int32, (rows, 1), 0)
    return jnp.logical_or(t > 0, r >= PADF)


def _dot(a, b):
    return jnp.dot(a, b, preferred_element_type=F32)


def _dot_nt(a, b):
    return lax.dot_general(a, b, (((1,), (1,)), ((), ())), preferred_element_type=F32)


def _dot_tn(a, b):
    return lax.dot_general(a, b, (((0,), (0,)), ((), ())), preferred_element_type=F32)


def _norm_matmul_kernel(*refs, n_out, scales, transposed, n_chunk):
    h_ref, g_ref = refs[0], refs[1]
    w_refs = refs[2:2 + n_out]
    o_refs = refs[2 + n_out:2 + 2 * n_out]
    xn = _rms(h_ref[0], g_ref[...]).astype(BF16)
    for w_ref, o_ref, scale, tr in zip(w_refs, o_refs, scales, transposed):
        n = w_ref.shape[-1]
        step = min(n_chunk, n)
        for n0 in range(0, n, step):
            acc = _dot(xn, w_ref[:, n0:n0 + step])
            if scale != 1.0:
                acc = acc * scale
            if tr:
                o_ref[0, n0:n0 + step, :] = acc.T.astype(o_ref.dtype)
            else:
                o_ref[0, :, n0:n0 + step] = acc.astype(o_ref.dtype)


def _norm_matmul(h, g, ws, out_dtypes, scales, transposed=None):
    b, lp, d = h.shape
    tm = ROW_TILE
    transposed = tuple(transposed) if transposed is not None else (False,) * len(ws)
    kern = functools.partial(_norm_matmul_kernel, n_out=len(ws), scales=tuple(scales),
                             transposed=transposed, n_chunk=512)
    out_specs, out_shape = [], []
    for w, dt, tr in zip(ws, out_dtypes, transposed):
        n = w.shape[1]
        if tr:
            out_specs.append(pl.BlockSpec((1, n, tm), lambda i, t: (i, 0, t)))
            out_shape.append(jax.ShapeDtypeStruct((b, n, lp), dt))
        else:
            out_specs.append(pl.BlockSpec((1, tm, n), lambda i, t: (i, t, 0)))
            out_shape.append(jax.ShapeDtypeStruct((b, lp, n), dt))
    return pl.pallas_call(
        kern,
        grid=(b, lp // tm),
        in_specs=[pl.BlockSpec((1, tm, d), lambda i, t: (i, t, 0)), _const_spec((1, d))]
        + [_const_spec(w.shape) for w in ws],
        out_specs=out_specs,
        out_shape=out_shape,
        compiler_params=_cparams(2),
        name="norm_matmul",
    )(h, g.reshape(1, d), *ws)


def _bdot(a, b):
    return _dot(a.astype(BF16), b.astype(BF16))


def _tri_inverse(mats, row, col):
    n = mats[0].shape[0]
    eye = (row == col).astype(F32)

    def same_block(s):
        sh = s.bit_length() - 1
        return lax.shift_right_logical(row, sh) == lax.shift_right_logical(col, sh)

    nb = [jnp.where(same_block(8), -a, 0.0).astype(BF16) for a in mats]
    n2 = [_dot(x, x) for x in nb]
    p = [eye + x.astype(F32) for x in nb]
    p = [x + _bdot(x, y) for x, y in zip(p, n2)]
    n4 = [_bdot(y, y) for y in n2]
    p = [x + _bdot(x, y) for x, y in zip(p, n4)]
    s = 8
    while s < n:
        pick = jnp.logical_and(same_block(2 * s), jnp.logical_not(same_block(s)))
        off = [jnp.where(pick, a, 0.0).astype(BF16) for a in mats]
        pb = [x.astype(BF16) for x in p]
        y = [_dot(o, x) for o, x in zip(off, pb)]
        p = [x - _bdot(xb, yy) for x, xb, yy in zip(p, pb, y)]
        s *= 2
    return p


def _gdn_kernel(x_ref, gt_ref, cw_ref, alog_ref, dtb_ref, og_ref, o_ref, state_ref, carry_ref):
    t = pl.program_id(1)
    c = CHUNK
    nh = GDN_HEADS
    dk = LANE
    qk_w = nh * dk

    @pl.when(t == 0)
    def _():
        state_ref[...] = jnp.zeros_like(state_ref)
        carry_ref[...] = jnp.zeros_like(carry_ref)

    x = x_ref[0, :, 0:3 * qk_w].astype(F32)
    ext = jnp.concatenate([carry_ref[...], x], axis=0)
    cw = cw_ref[...]
    conv = (cw[3:4] * x + cw[2:3] * ext[7:7 + c] + cw[1:2] * ext[6:6 + c] + cw[0:1] * ext[5:5 + c])
    carry_ref[...] = x[c - 8:c]
    qkv = conv * jax.nn.sigmoid(conv)

    keep = _keep_rows(t, c)
    gts = gt_ref[0]
    beta_all = jnp.where(keep, jax.nn.sigmoid(gts), 0.0)
    g_all = -jnp.exp(alog_ref[...]) * jax.nn.softplus(gts + dtb_ref[...])
    g_all = jnp.where(keep, g_all, 0.0)

    row = lax.broadcasted_iota(jnp.int32, (c, c), 0)
    col = lax.broadcasted_iota(jnp.int32, (c, c), 1)
    causal = col <= row
    strict = col < row
    tril = causal.astype(F32)
    triu = (row <= col).astype(F32)
    gc_all = jnp.dot(tril, g_all, preferred_element_type=F32, precision=lax.Precision.HIGHEST)
    gct_all = jnp.dot(g_all.T, triu, preferred_element_type=F32, precision=lax.Precision.HIGHEST)

    heads = range(nh)
    qn, kn, vb, kb, decay, egc, gcol = [], [], [], [], [], [], []
    for h in heads:
        qh = qkv[:, h * dk:(h + 1) * dk]
        kh = qkv[:, qk_w + h * dk:qk_w + (h + 1) * dk]
        vh = qkv[:, 2 * qk_w + h * dk:2 * qk_w + (h + 1) * dk]
        qn.append(qh * lax.rsqrt(jnp.sum(qh * qh, axis=-1, keepdims=True) + EPS) * (dk ** -0.5))
        kn.append(kh * lax.rsqrt(jnp.sum(kh * kh, axis=-1, keepdims=True) + EPS))
        beta = beta_all[:, h:h + 1]
        gcol.append(gc_all[:, nh + h:nh + h + 1])
        grow = gct_all[nh + h:nh + h + 1, :]
        decay.append(jnp.where(causal, jnp.exp(gcol[h] - grow), 0.0))
        egc.append(jnp.exp(gcol[h]))
        kb.append(kn[h] * beta)
        vb.append(vh * beta)
    kq = [_dot_nt(jnp.concatenate([kb[h], qn[h]], axis=0).astype(BF16), kn[h].astype(BF16))
          for h in heads]
    tinv = _tri_inverse([jnp.where(strict, kq[h][:c] * decay[h], 0.0) for h in heads], row, col)
    uw = [_bdot(tinv[h], jnp.concatenate([vb[h], kb[h] * egc[h]], axis=1)) for h in heads]
    state = [state_ref[h] for h in heads]
    ws_qs = [_bdot(jnp.concatenate([uw[h][:, dk:], qn[h] * egc[h]], axis=0), state[h]) for h in heads]
    v_new = [uw[h][:, :dk] - ws_qs[h][:c] for h in heads]
    o = [ws_qs[h][c:] + _bdot(kq[h][c:] * decay[h], v_new[h]) for h in heads]
    g_last = [gcol[h][c - 1:c] for h in heads]
    kv = [_dot_tn((kn[h] * jnp.exp(g_last[h] - gcol[h])).astype(BF16), v_new[h].astype(BF16)) for h in heads]
    for h in heads:
        state_ref[h] = state[h] * jnp.exp(g_last[h]) + kv[h]
        z = x_ref[0, :, 3 * qk_w + h * dk:3 * qk_w + (h + 1) * dk].astype(F32)
        on = _rms(o[h], og_ref[...])
        o_ref[0, :, h * dk:(h + 1) * dk] = (on * (z * jax.nn.sigmoid(z))).astype(o_ref.dtype)


def _gdn_core(proj, gates, conv_w, alog_row, dtb_row, o_gain):
    b, lp, width = proj.shape
    v_w = GDN_HEADS * LANE
    c = CHUNK
    return pl.pallas_call(
        _gdn_kernel,
        grid=(b, lp // c),
        in_specs=[pl.BlockSpec((1, c, width), lambda i, t: (i, t, 0)),
                  pl.BlockSpec((1, c, LANE), lambda i, t: (i, t, 0)),
                  _const_spec(conv_w.shape), _const_spec((1, LANE)), _const_spec((1, LANE)),
                  _const_spec((1, LANE))],
        out_specs=pl.BlockSpec((1, c, v_w), lambda i, t: (i, t, 0)),
        out_shape=jax.ShapeDtypeStruct((b, lp, v_w), BF16),
        scratch_shapes=[pltpu.VMEM((GDN_HEADS, LANE, LANE), F32),
                        pltpu.VMEM((8, 3 * v_w), F32)],
        compiler_params=_cparams(2),
        name="gdn_core",
    )(proj, gates, conv_w, alog_row, dtb_row, o_gain.reshape(1, LANE))


def _diff_attn_kernel(q_ref, k_ref, vt_ref, slope_ref, lq1_ref, lk1_ref, lq2_ref, lk2_ref, sub_ref,
                      o_ref, qq_ref, bias_ref, *chunk_refs, lam_init):
    i = pl.program_id(2)
    tq = q_ref.shape[1]
    tk = KV_TILE
    lp = k_ref.shape[1]
    dh = LANE // 2
    qc = Q_CHUNK
    n_chunks = 2 * tq // qc
    s_refs, m_refs, acc_refs = (chunk_refs[n::3] for n in range(3))

    q = q_ref[0]
    lane = lax.broadcasted_iota(jnp.int32, (1, LANE), 1)
    zero = jnp.zeros_like(q)
    qq_ref[0:tq, :] = jnp.where(lane < dh, q, zero)
    qq_ref[tq:2 * tq, :] = jnp.where(lane >= dh, q, zero)

    slope = slope_ref[0][:, 0:1] * LOG2E
    krel = lax.broadcasted_iota(jnp.int32, (tk, qc), 0)
    bias_ref[...] = slope * krel.astype(F32)
    ones_rows = jnp.ones((SUM_ROWS, tk), BF16)
    c2 = lax.broadcasted_iota(jnp.int32, (1, 2 * tq), 1)
    qpos = i * tq + jnp.where(c2 >= tq, c2 - tq, c2)

    def block_start(j):
        return pl.multiple_of(jnp.minimum(j * tk, lp - tk), LANE)

    def scores(k, ci):
        return _dot_nt(k, qq_ref[ci * qc:(ci + 1) * qc, :])

    k0 = k_ref[0, pl.ds(block_start(0), tk), :]
    for ci in range(n_chunks):
        s_refs[ci][...] = scores(k0, ci)
        m_refs[ci][...] = jnp.full_like(m_refs[ci], NEG)
        acc_refs[ci][...] = jnp.zeros_like(acc_refs[ci])

    def step(j, masked):
        start = block_start(j)
        k_next = k_ref[0, pl.ds(block_start(j + 1), tk), :]
        vt = jnp.concatenate([vt_ref[0, :, pl.ds(start, tk)], ones_rows], axis=0)
        c = slope * jnp.asarray(start, F32)
        bias = bias_ref[...]
        for ci in range(n_chunks):
            s = s_refs[ci][...] + bias
            s_next = scores(k_next, ci)
            if masked:
                kpos = start + krel
                valid = jnp.logical_and(kpos <= qpos[:, ci * qc:(ci + 1) * qc],
                                        kpos >= jnp.maximum(j * tk, PADF))
                s = jnp.where(valid, s, NEG)
            m_prev = m_refs[ci][...]
            m_new = jnp.maximum(m_prev, jnp.max(s, axis=0, keepdims=True) + c)
            alpha = jnp.exp2(m_prev - m_new)
            p = jnp.exp2((s - (m_new - c)).astype(BF16))
            acc_refs[ci][...] = alpha * acc_refs[ci][...] + _dot(vt, p)
            m_refs[ci][...] = m_new
            s_refs[ci][...] = s_next

    def masked_body(j, carry):
        step(j, True)
        return carry

    def plain_body(j, carry):
        step(j, False)
        return carry

    n_kv = ((i + 1) * tq + tk - 1) // tk
    n_plain_end = jnp.maximum((i * tq + 1) // tk, 1)
    step(0, True)
    lax.fori_loop(1, n_plain_end, plain_body, 0)
    lax.fori_loop(n_plain_end, n_kv, masked_body, 0)

    accs = [acc_refs[ci][...] for ci in range(n_chunks)]
    o = jnp.concatenate([a[0:LANE] / a[LANE:LANE + 1] for a in accs], axis=1)
    lam = (jnp.exp(jnp.sum(lq1_ref[...] * lk1_ref[...], axis=-1, keepdims=True))
           - jnp.exp(jnp.sum(lq2_ref[...] * lk2_ref[...], axis=-1, keepdims=True)) + lam_init)
    od = o[:, :tq] - lam * o[:, tq:]
    on = od * lax.rsqrt(jnp.mean(od * od, axis=0, keepdims=True) + EPS) * sub_ref[...]
    o_ref[0] = (on * (1.0 - lam_init)).T.astype(o_ref.dtype)


def _diff_attn(q, k, vt, slopes, lq1, lk1, lq2, lk2, subln, lam_init):
    b, lp, width = q.shape
    nh = DIFF_HEADS
    tq = ROW_TILE
    dh = lq1.shape[-1]
    kern = functools.partial(_diff_attn_kernel, lam_init=lam_init)
    vec = lambda a: a.reshape(1, dh).astype(F32)
    chunk_scratch = [pltpu.VMEM((KV_TILE, Q_CHUNK), F32), pltpu.VMEM((1, Q_CHUNK), F32),
                     pltpu.VMEM((LANE + SUM_ROWS, Q_CHUNK), F32)]
    return pl.pallas_call(
        kern,
        grid=(b, nh, lp // tq),
        in_specs=[pl.BlockSpec((1, tq, LANE), lambda bi, h, i: (bi, i, h)),
                  pl.BlockSpec((1, lp, LANE), lambda bi, h, i: (bi, 0, h)),
                  pl.BlockSpec((1, LANE, lp), lambda bi, h, i: (bi, h, 0)),
                  pl.BlockSpec((1, 1, LANE), lambda bi, h, i: (h, 0, 0)),
                  _const_spec((1, dh)), _const_spec((1, dh)), _const_spec((1, dh)), _const_spec((1, dh)),
                  _const_spec((LANE, 1))],
        out_specs=pl.BlockSpec((1, tq, LANE), lambda bi, h, i: (bi, i, h)),
        out_shape=jax.ShapeDtypeStruct((b, lp, width), BF16),
        scratch_shapes=[pltpu.VMEM((2 * tq, LANE), BF16), pltpu.VMEM((KV_TILE, Q_CHUNK), F32)]
        + chunk_scratch * (2 * tq // Q_CHUNK),
        compiler_params=_cparams(3),
        name="diff_attn",
    )(q, k, vt, slopes, vec(lq1), vec(lk1), vec(lq2), vec(lk2), subln.reshape(LANE, 1).astype(F32))


def _ffn_kernel(h_ref, y_ref, wo_ref, g_ref, wg_ref, wu_ref, cg_ref, cu_ref, wd_ref, fg_ref, o_ref,
                xn_ref, acc_ref, carry_g_ref, carry_u_ref, ug_a, uu_a, ug_b, uu_b, *, final):
    t = pl.program_id(1)
    tm = h_ref.shape[1]
    nf = wg_ref.shape[0]
    slot_a = (ug_a, uu_a)
    slot_b = (ug_b, uu_b)

    @pl.when(t == 0)
    def _():
        carry_g_ref[...] = jnp.zeros_like(carry_g_ref)
        carry_u_ref[...] = jnp.zeros_like(carry_u_ref)

    x = jnp.where(_keep_rows(t, tm), h_ref[0] + _dot(y_ref[0], wo_ref[...]), 0.0)
    xn_ref[...] = _rms(x, g_ref[...]).astype(BF16)
    acc_ref[...] = x

    rc = LANE

    def up_gate(f, slot):
        slot[0][8:8 + tm, :] = _dot(xn_ref[...], wg_ref[f])

    def up_up(f, slot):
        slot[1][8:8 + tm, :] = _dot(xn_ref[...], wu_ref[f])

    def load_halo(f, slot):
        for u_ref, carry_ref in ((slot[0], carry_g_ref), (slot[1], carry_u_ref)):
            u_ref[0:8, :] = carry_ref[f]
            carry_ref[f] = u_ref[tm:tm + 8, :]

    def conv(u_ref, cw, r0):
        return (cw[2:3] * u_ref[8 + r0:8 + r0 + rc, :] + cw[1:2] * u_ref[7 + r0:7 + r0 + rc, :]
                + cw[0:1] * u_ref[6 + r0:6 + r0 + rc, :])

    def act_down(f, slot, r0):
        gate = conv(slot[0], cg_ref[f], r0)
        up = conv(slot[1], cu_ref[f], r0)
        act = (gate * jax.nn.sigmoid(gate) * up).astype(BF16)
        acc_ref[r0:r0 + rc, :] += _dot(act, wd_ref[f])

    def stage(f, cur, other, last=False):
        load_halo(f, cur)
        if not last:
            up_gate(f + 1, other)
        act_down(f, cur, 0)
        if not last:
            up_up(f + 1, other)
        for r0 in range(rc, tm, rc):
            act_down(f, cur, r0)

    def pair(p, carry):
        f = 2 * p
        stage(f, slot_a, slot_b)
        stage(f + 1, slot_b, slot_a)
        return carry

    up_gate(0, slot_a)
    up_up(0, slot_a)
    lax.fori_loop(0, (nf - 1) // 2, pair, 0)
    stage(nf - 1, slot_a, slot_b, last=True)
    out = acc_ref[...]
    if final:
        out = _rms(out, fg_ref[...])
    o_ref[0] = jnp.where(_keep_rows(t, tm), out, 0.0)


def _ffn(h, y, wo, g, wg, wu, cg, cu, wd, final_gain=None):
    b, lp, d = h.shape
    tm = ROW_TILE
    nf, _, tf = wg.shape
    final = final_gain is not None
    fg = (final_gain if final else g).reshape(1, d)
    return pl.pallas_call(
        functools.partial(_ffn_kernel, final=final),
        grid=(b, lp // tm),
        in_specs=[pl.BlockSpec((1, tm, d), lambda i, t: (i, t, 0)),
                  pl.BlockSpec((1, tm, y.shape[2]), lambda i, t: (i, t, 0)),
                  _const_spec(wo.shape), _const_spec((1, d)),
                  _const_spec(wg.shape), _const_spec(wu.shape), _const_spec(cg.shape),
                  _const_spec(cu.shape), _const_spec(wd.shape), _const_spec((1, d))],
        out_specs=pl.BlockSpec((1, tm, d), lambda i, t: (i, t, 0)),
        out_shape=jax.ShapeDtypeStruct(h.shape, h.dtype),
        scratch_shapes=[pltpu.VMEM((tm, d), BF16), pltpu.VMEM((tm, d), F32),
                        pltpu.VMEM((nf, 8, tf), F32), pltpu.VMEM((nf, 8, tf), F32)]
        + [pltpu.VMEM((tm + 8, tf), F32)] * 4,
        input_output_aliases={0: 0},
        compiler_params=_cparams(2),
        name="conv_ffn",
    )(h, y, wo, g.reshape(1, d), wg, wu, cg, cu, wd, fg)


def _lane_row(vals, offset):
    return jnp.zeros((1, LANE), F32).at[0, offset:offset + vals.shape[0]].set(vals.astype(F32))


def kernel(x, meta_tokens, a_norm, a_w_in, a_conv, a_log, a_dt_bias, a_onorm, a_w_o, kv_norm, w_kv, lambda_k1, lambda_k2, b_norm, b_w_q, b_lambda_q1, b_lambda_q2, b_subln, b_w_o, ffn_norm, ffn_w_up, ffn_conv, ffn_w_down, final_norm):
    bn, seq, d = x.shape
    n_a = a_w_in.shape[0]
    n_b = b_w_q.shape[0]
    depth = n_a + n_b
    nh = GDN_HEADS
    qkvz_w = 4 * nh * LANE
    assert d == nh * LANE and seq % LANE == 0 and (LANE + seq) % ROW_TILE == 0
    d_ff = ffn_w_down.shape[1]
    assert d_ff % FF_TILE == 0
    nf = d_ff // FF_TILE
    assert nf >= 3 and nf % 2 == 1

    meta = jnp.broadcast_to(meta_tokens.astype(x.dtype)[None], (bn, N_META, d))
    h = jnp.concatenate([jnp.zeros((bn, PADF, d), x.dtype), meta, x], axis=1)

    def ffn_layer(h, mixer_out, w_o, layer):
        w_up = ffn_w_up[layer].astype(BF16).reshape(d, 2, nf, FF_TILE).transpose(1, 2, 0, 3)
        cw = ffn_conv[layer].astype(F32).reshape(FFN_CONV, 2, nf, FF_TILE).transpose(1, 2, 0, 3)
        w_dn = ffn_w_down[layer].astype(BF16).reshape(nf, FF_TILE, d)
        return _ffn(h, mixer_out, w_o.astype(BF16), ffn_norm[layer], w_up[0], w_up[1], cw[0], cw[1], w_dn,
                    final_gain=final_norm if layer == depth - 1 else None)

    for layer in range(n_a):
        w_in = a_w_in[layer]
        w_main = w_in[:, :qkvz_w].astype(BF16)
        w_gate = jnp.pad(w_in[:, qkvz_w:], ((0, 0), (0, LANE - 2 * nh))).astype(BF16)
        proj, gates = _norm_matmul(h, a_norm[layer], [w_main, w_gate], [BF16, F32], [1.0, 1.0])
        o = _gdn_core(proj, gates, a_conv[layer].astype(F32), _lane_row(a_log[layer], nh),
                      _lane_row(a_dt_bias[layer], nh), a_onorm[layer].astype(F32))
        h = ffn_layer(h, o, a_w_o[layer], layer)

    diff_w = DIFF_HEADS * LANE
    k_sh, vt_sh = _norm_matmul(h, kv_norm, [w_kv[:, :diff_w].astype(BF16), w_kv[:, diff_w:].astype(BF16)],
                               [BF16, BF16], [1.0, 1.0], transposed=[False, True])
    slopes = jnp.exp2(-(8.0 / DIFF_HEADS) * jnp.arange(1, DIFF_HEADS + 1, dtype=F32))
    slopes = jnp.broadcast_to(slopes[:, None, None], (DIFF_HEADS, 1, LANE))
    dh = lambda_k1.shape[0]
    for j in range(n_b):
        layer = n_a + j
        lam_init = 0.8 - 0.6 * math.exp(-0.3 * layer)
        (q,) = _norm_matmul(h, b_norm[j], [b_w_q[j].astype(BF16)], [BF16], [dh ** -0.5 * LOG2E])
        o = _diff_attn(q, k_sh, vt_sh, slopes, b_lambda_q1[j], lambda_k1, b_lambda_q2[j], lambda_k2,
                       b_subln[j], lam_init)
        h = ffn_layer(h, o, b_w_o[j], layer)

    return h[:, LANE:]
```

```python
import functools
import math

import jax
import jax.numpy as jnp
from jax import lax
from jax.experimental import pallas as pl
from jax.experimental.pallas import tpu as pltpu

N_META = 16
GDN_HEADS = 8
GDN_CONV = 4
DIFF_HEADS = 8
FFN_CONV = 3
EPS = 1e-6

LANE = 128
PADF = LANE - N_META
CHUNK = 128
ROW_TILE = 384
KV_TILE = 512
Q_CHUNK = 256
FF_TILE = 256
SUM_ROWS = 16
LOG2E = math.log2(math.e)
NEG = -1e30
VMEM_LIMIT = 56 * 1024 * 1024

F32 = jnp.float32
BF16 = jnp.bfloat16


def _cparams(n_axes):
    return pltpu.CompilerParams(
        dimension_semantics=("arbitrary",) * n_axes,
        vmem_limit_bytes=VMEM_LIMIT,
    )


def _const_spec(shape):
    zeros = (0,) * len(shape)
    return pl.BlockSpec(shape, lambda *_: zeros, pipeline_mode=pl.Buffered(1))


def _rms(x, g):
    return x * lax.rsqrt(jnp.mean(x * x, axis=-1, keepdims=True) + EPS) * g


def _keep_rows(t, rows):
    r = lax.broadcasted_iota(jnp.int32, (rows, 1), 0)
    return jnp.logical_or(t > 0, r >= PADF)


def _dot(a, b):
    return jnp.dot(a, b, preferred_element_type=F32)


def _dot_nt(a, b):
    return lax.dot_general(a, b, (((1,), (1,)), ((), ())), preferred_element_type=F32)


def _dot_tn(a, b):
    return lax.dot_general(a, b, (((0,), (0,)), ((), ())), preferred_element_type=F32)


def _norm_matmul_kernel(*refs, n_out, scales, transposed, n_chunk):
    h_ref, g_ref = refs[0], refs[1]
    w_refs = refs[2:2 + n_out]
    o_refs = refs[2 + n_out:2 + 2 * n_out]
    xn = _rms(h_ref[0], g_ref[...]).astype(BF16)
    for w_ref, o_ref, scale, tr in zip(w_refs, o_refs, scales, transposed):
        n = w_ref.shape[-1]
        step = min(n_chunk, n)
        for n0 in range(0, n, step):
            acc = _dot(xn, w_ref[:, n0:n0 + step])
            if scale != 1.0:
                acc = acc * scale
            if tr:
                o_ref[0, n0:n0 + step, :] = acc.T.astype(o_ref.dtype)
            else:
                o_ref[0, :, n0:n0 + step] = acc.astype(o_ref.dtype)


def _norm_matmul(h, g, ws, out_dtypes, scales, transposed=None):
    b, lp, d = h.shape
    tm = ROW_TILE
    transposed = tuple(transposed) if transposed is not None else (False,) * len(ws)
    kern = functools.partial(_norm_matmul_kernel, n_out=len(ws), scales=tuple(scales),
                             transposed=transposed, n_chunk=512)
    out_specs, out_shape = [], []
    for w, dt, tr in zip(ws, out_dtypes, transposed):
        n = w.shape[1]
        if tr:
            out_specs.append(pl.BlockSpec((1, n, tm), lambda i, t: (i, 0, t)))
            out_shape.append(jax.ShapeDtypeStruct((b, n, lp), dt))
        else:
            out_specs.append(pl.BlockSpec((1, tm, n), lambda i, t: (i, t, 0)))
            out_shape.append(jax.ShapeDtypeStruct((b, lp, n), dt))
    return pl.pallas_call(
        kern,
        grid=(b, lp // tm),
        in_specs=[pl.BlockSpec((1, tm, d), lambda i, t: (i, t, 0)), _const_spec((1, d))]
        + [_const_spec(w.shape) for w in ws],
        out_specs=out_specs,
        out_shape=out_shape,
        compiler_params=_cparams(2),
        name="norm_matmul",
    )(h, g.reshape(1, d), *ws)


def _bdot(a, b):
    return _dot(a.astype(BF16), b.astype(BF16))


def _tri_inverse(mats, row, col):
    n = mats[0].shape[0]
    eye = (row == col).astype(F32)

    def same_block(s):
        sh = s.bit_length() - 1
        return lax.shift_right_logical(row, sh) == lax.shift_right_logical(col, sh)

    nb = [jnp.where(same_block(8), -a, 0.0).astype(BF16) for a in mats]
    n2 = [_dot(x, x) for x in nb]
    p = [eye + x.astype(F32) for x in nb]
    p = [x + _bdot(x, y) for x, y in zip(p, n2)]
    n4 = [_bdot(y, y) for y in n2]
    p = [x + _bdot(x, y) for x, y in zip(p, n4)]
    s = 8
    while s < n:
        pick = jnp.logical_and(same_block(2 * s), jnp.logical_not(same_block(s)))
        off = [jnp.where(pick, a, 0.0).astype(BF16) for a in mats]
        pb = [x.astype(BF16) for x in p]
        y = [_dot(o, x) for o, x in zip(off, pb)]
        p = [x - _bdot(xb, yy) for x, xb, yy in zip(p, pb, y)]
        s *= 2
    return p


def _gdn_kernel(x_ref, gt_ref, cw_ref, alog_ref, dtb_ref, og_ref, o_ref, state_ref, carry_ref):
    t = pl.program_id(1)
    c = CHUNK
    nh = GDN_HEADS
    dk = LANE
    qk_w = nh * dk

    @pl.when(t == 0)
    def _():
        state_ref[...] = jnp.zeros_like(state_ref)
        carry_ref[...] = jnp.zeros_like(carry_ref)

    x = x_ref[0, :, 0:3 * qk_w].astype(F32)
    ext = jnp.concatenate([carry_ref[...], x], axis=0)
    cw = cw_ref[...]
    conv = (cw[3:4] * x + cw[2:3] * ext[7:7 + c] + cw[1:2] * ext[6:6 + c] + cw[0:1] * ext[5:5 + c])
    carry_ref[...] = x[c - 8:c]
    qkv = conv * jax.nn.sigmoid(conv)

    keep = _keep_rows(t, c)
    gts = gt_ref[0]
    beta_all = jnp.where(keep, jax.nn.sigmoid(gts), 0.0)
    g_all = -jnp.exp(alog_ref[...]) * jax.nn.softplus(gts + dtb_ref[...])
    g_all = jnp.where(keep, g_all, 0.0)

    row = lax.broadcasted_iota(jnp.int32, (c, c), 0)
    col = lax.broadcasted_iota(jnp.int32, (c, c), 1)
    causal = col <= row
    strict = col < row
    tril = causal.astype(F32)
    triu = (row <= col).astype(F32)
    gc_all = jnp.dot(tril, g_all, preferred_element_type=F32, precision=lax.Precision.HIGHEST)
    gct_all = jnp.dot(g_all.T, triu, preferred_element_type=F32, precision=lax.Precision.HIGHEST)

    heads = range(nh)
    qn, kn, vb, kb, decay, egc, gcol = [], [], [], [], [], [], []
    for h in heads:
        qh = qkv[:, h * dk:(h + 1) * dk]
        kh = qkv[:, qk_w + h * dk:qk_w + (h + 1) * dk]
        vh = qkv[:, 2 * qk_w + h * dk:2 * qk_w + (h + 1) * dk]
        qn.append(qh * lax.rsqrt(jnp.sum(qh * qh, axis=-1, keepdims=True) + EPS) * (dk ** -0.5))
        kn.append(kh * lax.rsqrt(jnp.sum(kh * kh, axis=-1, keepdims=True) + EPS))
        beta = beta_all[:, h:h + 1]
        gcol.append(gc_all[:, nh + h:nh + h + 1])
        grow = gct_all[nh + h:nh + h + 1, :]
        decay.append(jnp.where(causal, jnp.exp(gcol[h] - grow), 0.0))
        egc.append(jnp.exp(gcol[h]))
        kb.append(kn[h] * beta)
        vb.append(vh * beta)
    kq = [_dot_nt(jnp.concatenate([kb[h], qn[h]], axis=0).astype(BF16), kn[h].astype(BF16))
          for h in heads]
    tinv = _tri_inverse([jnp.where(strict, kq[h][:c] * decay[h], 0.0) for h in heads], row, col)
    uw = [_bdot(tinv[h], jnp.concatenate([vb[h], kb[h] * egc[h]], axis=1)) for h in heads]
    state = [state_ref[h] for h in heads]
    ws_qs = [_bdot(jnp.concatenate([uw[h][:, dk:], qn[h] * egc[h]], axis=0), state[h]) for h in heads]
    v_new = [uw[h][:, :dk] - ws_qs[h][:c] for h in heads]
    o = [ws_qs[h][c:] + _bdot(kq[h][c:] * decay[h], v_new[h]) for h in heads]
    g_last = [gcol[h][c - 1:c] for h in heads]
    kv = [_dot_tn((kn[h] * jnp.exp(g_last[h] - gcol[h])).astype(BF16), v_new[h].astype(BF16)) for h in heads]
    for h in heads:
        state_ref[h] = state[h] * jnp.exp(g_last[h]) + kv[h]
        z = x_ref[0, :, 3 * qk_w + h * dk:3 * qk_w + (h + 1) * dk].astype(F32)
        on = _rms(o[h], og_ref[...])
        o_ref[0, :, h * dk:(h + 1) * dk] = (on * (z * jax.nn.sigmoid(z))).astype(o_ref.dtype)


def _gdn_core(proj, gates, conv_w, alog_row, dtb_row, o_gain):
    b, lp, width = proj.shape
    v_w = GDN_HEADS * LANE
    c = CHUNK
    return pl.pallas_call(
        _gdn_kernel,
        grid=(b, lp // c),
        in_specs=[pl.BlockSpec((1, c, width), lambda i, t: (i, t, 0)),
                  pl.BlockSpec((1, c, LANE), lambda i, t: (i, t, 0)),
                  _const_spec(conv_w.shape), _const_spec((1, LANE)), _const_spec((1, LANE)),
                  _const_spec((1, LANE))],
        out_specs=pl.BlockSpec((1, c, v_w), lambda i, t: (i, t, 0)),
        out_shape=jax.ShapeDtypeStruct((b, lp, v_w), BF16),
        scratch_shapes=[pltpu.VMEM((GDN_HEADS, LANE, LANE), F32),
                        pltpu.VMEM((8, 3 * v_w), F32)],
        compiler_params=_cparams(2),
        name="gdn_core",
    )(proj, gates, conv_w, alog_row, dtb_row, o_gain.reshape(1, LANE))


def _diff_attn_kernel(q_ref, k_ref, vt_ref, slope_ref, lq1_ref, lk1_ref, lq2_ref, lk2_ref, sub_ref,
                      o_ref, qq_ref, bias_ref, *chunk_refs, lam_init):
    i = pl.program_id(2)
    tq = q_ref.shape[1]
    tk = KV_TILE
    lp = k_ref.shape[1]
    dh = LANE // 2
    qc = Q_CHUNK
    n_chunks = 2 * tq // qc
    s_refs, m_refs, acc_refs = (chunk_refs[n::3] for n in range(3))

    q = q_ref[0]
    lane = lax.broadcasted_iota(jnp.int32, (1, LANE), 1)
    zero = jnp.zeros_like(q)
    qq_ref[0:tq, :] = jnp.where(lane < dh, q, zero)
    qq_ref[tq:2 * tq, :] = jnp.where(lane >= dh, q, zero)

    slope = slope_ref[0][:, 0:1] * LOG2E
    krel = lax.broadcasted_iota(jnp.int32, (tk, qc), 0)
    @pl.when(i == 0)
    def _():
        bias_ref[...] = slope * krel.astype(F32)

    ones_rows = jnp.ones((SUM_ROWS, tk), BF16)
    c2 = lax.broadcasted_iota(jnp.int32, (1, 2 * tq), 1)
    qpos = i * tq + jnp.where(c2 >= tq, c2 - tq, c2)

    def block_start(j):
        return pl.multiple_of(jnp.minimum(j * tk, lp - tk), LANE)

    def scores(k, ci):
        return _dot_nt(k, qq_ref[ci * qc:(ci + 1) * qc, :])

    k0 = k_ref[0, pl.ds(block_start(0), tk), :]
    for ci in range(n_chunks):
        s_refs[ci][...] = scores(k0, ci)
        m_refs[ci][...] = jnp.full_like(m_refs[ci], NEG)
        acc_refs[ci][...] = jnp.zeros_like(acc_refs[ci])

    def step(j, masked):
        start = block_start(j)
        k_next = k_ref[0, pl.ds(block_start(j + 1), tk), :]
        vt = jnp.concatenate([vt_ref[0, :, pl.ds(start, tk)], ones_rows], axis=0)
        c = slope * start.astype(F32)
        bias = bias_ref[...]
        for ci in range(n_chunks):
            s = s_refs[ci][...] + bias
            s_next = scores(k_next, ci)
            if masked:
                kpos = start + krel
                valid = jnp.logical_and(kpos <= qpos[:, ci * qc:(ci + 1) * qc],
                                        kpos >= jnp.maximum(j * tk, PADF))
                s = jnp.where(valid, s, NEG)
            m_prev = m_refs[ci][...]
            m_new = jnp.maximum(m_prev, jnp.max(s, axis=0, keepdims=True) + c)
            alpha = jnp.exp2(m_prev - m_new)
            p = jnp.exp2((s - (m_new - c)).astype(BF16))
            acc_refs[ci][...] = alpha * acc_refs[ci][...] + _dot(vt, p)
            m_refs[ci][...] = m_new
            s_refs[ci][...] = s_next

    def run_blocks(lo, hi, masked):
        def pair(p, carry):
            step(lo + 2 * p, masked)
            step(lo + 2 * p + 1, masked)
            return carry

        n = jnp.maximum(hi - lo, 0)
        lax.fori_loop(0, n // 2, pair, 0)

        @pl.when(n % 2 == 1)
        def _():
            step(hi - 1, masked)

    n_kv = ((i + 1) * tq + tk - 1) // tk
    n_plain_end = jnp.maximum((i * tq + 1) // tk, 1)
    step(0, True)
    run_blocks(1, n_plain_end, False)
    run_blocks(n_plain_end, n_kv, True)

    accs = [acc_refs[ci][...] for ci in range(n_chunks)]
    o = jnp.concatenate([a[0:LANE] / a[LANE:LANE + 1] for a in accs], axis=1)
    lam = (jnp.exp(jnp.sum(lq1_ref[...] * lk1_ref[...], axis=-1, keepdims=True))
           - jnp.exp(jnp.sum(lq2_ref[...] * lk2_ref[...], axis=-1, keepdims=True)) + lam_init)
    od = o[:, :tq] - lam * o[:, tq:]
    on = od * lax.rsqrt(jnp.mean(od * od, axis=0, keepdims=True) + EPS) * sub_ref[...]
    o_ref[0] = (on * (1.0 - lam_init)).T.astype(o_ref.dtype)


def _diff_attn(q, k, vt, slopes, lq1, lk1, lq2, lk2, subln, lam_init):
    b, lp, width = q.shape
    nh = DIFF_HEADS
    tq = ROW_TILE
    dh = lq1.shape[-1]
    kern = functools.partial(_diff_attn_kernel, lam_init=lam_init)
    vec = lambda a: a.reshape(1, dh).astype(F32)
    chunk_scratch = [pltpu.VMEM((KV_TILE, Q_CHUNK), F32), pltpu.VMEM((1, Q_CHUNK), F32),
                     pltpu.VMEM((LANE + SUM_ROWS, Q_CHUNK), F32)]
    return pl.pallas_call(
        kern,
        grid=(b, nh, lp // tq),
        in_specs=[pl.BlockSpec((1, tq, LANE), lambda bi, h, i: (bi, i, h)),
                  pl.BlockSpec((1, lp, LANE), lambda bi, h, i: (bi, 0, h)),
                  pl.BlockSpec((1, LANE, lp), lambda bi, h, i: (bi, h, 0)),
                  pl.BlockSpec((1, 1, LANE), lambda bi, h, i: (h, 0, 0)),
                  _const_spec((1, dh)), _const_spec((1, dh)), _const_spec((1, dh)), _const_spec((1, dh)),
                  _const_spec((LANE, 1))],
        out_specs=pl.BlockSpec((1, tq, LANE), lambda bi, h, i: (bi, i, h)),
        out_shape=jax.ShapeDtypeStruct((b, lp, width), BF16),
        scratch_shapes=[pltpu.VMEM((2 * tq, LANE), BF16), pltpu.VMEM((KV_TILE, Q_CHUNK), F32)]
        + chunk_scratch * (2 * tq // Q_CHUNK),
        compiler_params=_cparams(3),
        name="diff_attn",
    )(q, k, vt, slopes, vec(lq1), vec(lk1), vec(lq2), vec(lk2), subln.reshape(LANE, 1).astype(F32))


def _ffn_kernel(h_ref, y_ref, wo_ref, g_ref, wg_ref, wu_ref, cg_ref, cu_ref, wd_ref, fg_ref, o_ref,
                xn_ref, acc_ref, carry_g_ref, carry_u_ref, ug_a, uu_a, ug_b, uu_b, *, final):
    t = pl.program_id(1)
    tm = h_ref.shape[1]
    nf = wg_ref.shape[0]
    slot_a = (ug_a, uu_a)
    slot_b = (ug_b, uu_b)

    @pl.when(t == 0)
    def _():
        carry_g_ref[...] = jnp.zeros_like(carry_g_ref)
        carry_u_ref[...] = jnp.zeros_like(carry_u_ref)

    x = jnp.where(_keep_rows(t, tm), h_ref[0] + _dot(y_ref[0], wo_ref[...]), 0.0)
    xn_ref[...] = _rms(x, g_ref[...]).astype(BF16)
    acc_ref[...] = x

    rc = LANE

    def up_gate(f, slot):
        slot[0][8:8 + tm, :] = _dot(xn_ref[...], wg_ref[f])

    def up_up(f, slot):
        slot[1][8:8 + tm, :] = _dot(xn_ref[...], wu_ref[f])

    def load_halo(f, slot):
        for u_ref, carry_ref in ((slot[0], carry_g_ref), (slot[1], carry_u_ref)):
            u_ref[0:8, :] = carry_ref[f]
            carry_ref[f] = u_ref[tm:tm + 8, :]

    def conv(u_ref, cw, r0):
        return (cw[2:3] * u_ref[8 + r0:8 + r0 + rc, :] + cw[1:2] * u_ref[7 + r0:7 + r0 + rc, :]
                + cw[0:1] * u_ref[6 + r0:6 + r0 + rc, :])

    def act_down(f, slot, r0):
        gate = conv(slot[0], cg_ref[f], r0)
        up = conv(slot[1], cu_ref[f], r0)
        act = (gate * jax.nn.sigmoid(gate) * up).astype(BF16)
        acc_ref[r0:r0 + rc, :] += _dot(act, wd_ref[f])

    def stage(f, cur, other, last=False):
        load_halo(f, cur)
        if not last:
            up_gate(f + 1, other)
        act_down(f, cur, 0)
        if not last:
            up_up(f + 1, other)
        for r0 in range(rc, tm, rc):
            act_down(f, cur, r0)

    up_gate(0, slot_a)
    up_up(0, slot_a)
    for f in range(0, nf - 1, 2):
        stage(f, slot_a, slot_b)
        stage(f + 1, slot_b, slot_a)
    stage(nf - 1, slot_a, slot_b, last=True)
    out = acc_ref[...]
    if final:
        out = _rms(out, fg_ref[...])
    o_ref[0] = jnp.where(_keep_rows(t, tm), out, 0.0)


def _ffn(h, y, wo, g, wg, wu, cg, cu, wd, final_gain=None):
    b, lp, d = h.shape
    tm = ROW_TILE
    nf, _, tf = wg.shape
    final = final_gain is not None
    fg = (final_gain if final else g).reshape(1, d)
    return pl.pallas_call(
        functools.partial(_ffn_kernel, final=final),
        grid=(b, lp // tm),
        in_specs=[pl.BlockSpec((1, tm, d), lambda i, t: (i, t, 0)),
                  pl.BlockSpec((1, tm, y.shape[2]), lambda i, t: (i, t, 0)),
                  _const_spec(wo.shape), _const_spec((1, d)),
                  _const_spec(wg.shape), _const_spec(wu.shape), _const_spec(cg.shape),
                  _const_spec(cu.shape), _const_spec(wd.shape), _const_spec((1, d))],
        out_specs=pl.BlockSpec((1, tm, d), lambda i, t: (i, t, 0)),
        out_shape=jax.ShapeDtypeStruct(h.shape, h.dtype),
        scratch_shapes=[pltpu.VMEM((tm, d), BF16), pltpu.VMEM((tm, d), F32),
                        pltpu.VMEM((nf, 8, tf), F32), pltpu.VMEM((nf, 8, tf), F32)]
        + [pltpu.VMEM((tm + 8, tf), F32)] * 4,
        input_output_aliases={0: 0},
        compiler_params=_cparams(2),
        name="conv_ffn",
    )(h, y, wo, g.reshape(1, d), wg, wu, cg, cu, wd, fg)


def _lane_row(vals, offset):
    return jnp.zeros((1, LANE), F32).at[0, offset:offset + vals.shape[0]].set(vals.astype(F32))


def kernel(x, meta_tokens, a_norm, a_w_in, a_conv, a_log, a_dt_bias, a_onorm, a_w_o, kv_norm, w_kv, lambda_k1, lambda_k2, b_norm, b_w_q, b_lambda_q1, b_lambda_q2, b_subln, b_w_o, ffn_norm, ffn_w_up, ffn_conv, ffn_w_down, final_norm):
    bn, seq, d = x.shape
    n_a = a_w_in.shape[0]
    n_b = b_w_q.shape[0]
    depth = n_a + n_b
    nh = GDN_HEADS
    qkvz_w = 4 * nh * LANE
    assert d == nh * LANE and seq % LANE == 0 and (LANE + seq) % ROW_TILE == 0
    d_ff = ffn_w_down.shape[1]
    assert d_ff % FF_TILE == 0
    nf = d_ff // FF_TILE
    assert nf >= 3 and nf % 2 == 1

    meta = jnp.broadcast_to(meta_tokens.astype(x.dtype)[None], (bn, N_META, d))
    h = jnp.concatenate([jnp.zeros((bn, PADF, d), x.dtype), meta, x], axis=1)

    def ffn_layer(h, mixer_out, w_o, layer):
        w_up = ffn_w_up[layer].astype(BF16).reshape(d, 2, nf, FF_TILE).transpose(1, 2, 0, 3)
        cw = ffn_conv[layer].astype(F32).reshape(FFN_CONV, 2, nf, FF_TILE).transpose(1, 2, 0, 3)
        w_dn = ffn_w_down[layer].astype(BF16).reshape(nf, FF_TILE, d)
        return _ffn(h, mixer_out, w_o.astype(BF16), ffn_norm[layer], w_up[0], w_up[1], cw[0], cw[1], w_dn,
                    final_gain=final_norm if layer == depth - 1 else None)

    for layer in range(n_a):
        w_in = a_w_in[layer]
        w_main = w_in[:, :qkvz_w].astype(BF16)
        w_gate = jnp.pad(w_in[:, qkvz_w:], ((0, 0), (0, LANE - 2 * nh))).astype(BF16)
        proj, gates = _norm_matmul(h, a_norm[layer], [w_main, w_gate], [BF16, F32], [1.0, 1.0])
        o = _gdn_core(proj, gates, a_conv[layer].astype(F32), _lane_row(a_log[layer], nh),
                      _lane_row(a_dt_bias[layer], nh), a_onorm[layer].astype(F32))
        h = ffn_layer(h, o, a_w_o[layer], layer)

    diff_w = DIFF_HEADS * LANE
    k_sh, vt_sh = _norm_matmul(h, kv_norm, [w_kv[:, :diff_w].astype(BF16), w_kv[:, diff_w:].astype(BF16)],
                               [BF16, BF16], [1.0, 1.0], transposed=[False, True])
    slopes = jnp.exp2(-(8.0 / DIFF_HEADS) * jnp.arange(1, DIFF_HEADS + 1, dtype=F32))
    slopes = jnp.broadcast_to(slopes[:, None, None], (DIFF_HEADS, 1, LANE))
    dh = lambda_k1.shape[0]
    for j in range(n_b):
        layer = n_a + j
        lam_init = 0.8 - 0.6 * math.exp(-0.3 * layer)
        (q,) = _norm_matmul(h, b_norm[j], [b_w_q[j].astype(BF16)], [BF16], [dh ** -0.5 * LOG2E])
        o = _diff_attn(q, k_sh, vt_sh, slopes, b_lambda_q1[j], lambda_k1, b_lambda_q2[j], lambda_k2,
                       b_subln[j], lam_init)
        h = ffn_layer(h, o, b_w_o[j], layer)

    return h[:, LANE:]
```

```python
import functools
import math

import jax
import jax.numpy as jnp
from jax import lax
from jax.experimental import pallas as pl
from jax.experimental.pallas import tpu as pltpu

N_META = 16
GDN_HEADS = 8
GDN_CONV = 4
DIFF_HEADS = 8
FFN_CONV = 3
EPS = 1e-6

LANE = 128
PADF = LANE - N_META
CHUNK = 128
ROW_TILE = 384
KV_TILE = 512
Q_CHUNK = 256
FF_TILE = 256
SUM_ROWS = 16
LOG2E = math.log2(math.e)
NEG = -1e30
VMEM_LIMIT = 56 * 1024 * 1024

F32 = jnp.float32
BF16 = jnp.bfloat16


def _cparams(n_axes):
    return pltpu.CompilerParams(
        dimension_semantics=("arbitrary",) * n_axes,
        vmem_limit_bytes=VMEM_LIMIT,
    )


def _const_spec(shape):
    zeros = (0,) * len(shape)
    return pl.BlockSpec(shape, lambda *_: zeros, pipeline_mode=pl.Buffered(1))


def _rms(x, g):
    return x * lax.rsqrt(jnp.mean(x * x, axis=-1, keepdims=True) + EPS) * g


def _keep_rows(t, rows):
    r = lax.broadcasted_iota(jnp.int32, (rows, 1), 0)
    return jnp.logical_or(t > 0, r >= PADF)


def _dot(a, b):
    return jnp.dot(a, b, preferred_element_type=F32)


def _dot_nt(a, b):
    return lax.dot_general(a, b, (((1,), (1,)), ((), ())), preferred_element_type=F32)


def _dot_tn(a, b):
    return lax.dot_general(a, b, (((0,), (0,)), ((), ())), preferred_element_type=F32)


def _norm_matmul_kernel(*refs, n_out, scales, transposed, n_chunk):
    h_ref, g_ref = refs[0], refs[1]
    w_refs = refs[2:2 + n_out]
    o_refs = refs[2 + n_out:2 + 2 * n_out]
    xn = _rms(h_ref[0], g_ref[...]).astype(BF16)
    for w_ref, o_ref, scale, tr in zip(w_refs, o_refs, scales, transposed):
        n = w_ref.shape[-1]
        step = min(n_chunk, n)
        for n0 in range(0, n, step):
            acc = _dot(xn, w_ref[:, n0:n0 + step])
            if scale != 1.0:
                acc = acc * scale
            if tr:
                o_ref[0, n0:n0 + step, :] = acc.T.astype(o_ref.dtype)
            else:
                o_ref[0, :, n0:n0 + step] = acc.astype(o_ref.dtype)


def _norm_matmul(h, g, ws, out_dtypes, scales, transposed=None):
    b, lp, d = h.shape
    tm = ROW_TILE
    transposed = tuple(transposed) if transposed is not None else (False,) * len(ws)
    kern = functools.partial(_norm_matmul_kernel, n_out=len(ws), scales=tuple(scales),
                             transposed=transposed, n_chunk=512)
    out_specs, out_shape = [], []
    for w, dt, tr in zip(ws, out_dtypes, transposed):
        n = w.shape[1]
        if tr:
            out_specs.append(pl.BlockSpec((1, n, tm), lambda i, t: (i, 0, t)))
            out_shape.append(jax.ShapeDtypeStruct((b, n, lp), dt))
        else:
            out_specs.append(pl.BlockSpec((1, tm, n), lambda i, t: (i, t, 0)))
            out_shape.append(jax.ShapeDtypeStruct((b, lp, n), dt))
    return pl.pallas_call(
        kern,
        grid=(b, lp // tm),
        in_specs=[pl.BlockSpec((1, tm, d), lambda i, t: (i, t, 0)), _const_spec((1, d))]
        + [_const_spec(w.shape) for w in ws],
        out_specs=out_specs,
        out_shape=out_shape,
        compiler_params=_cparams(2),
        name="norm_matmul",
    )(h, g.reshape(1, d), *ws)


def _bdot(a, b):
    return _dot(a.astype(BF16), b.astype(BF16))


def _tri_inverse(mats, row, col):
    n = mats[0].shape[0]
    eye = (row == col).astype(F32)

    def same_block(s):
        sh = s.bit_length() - 1
        return lax.shift_right_logical(row, sh) == lax.shift_right_logical(col, sh)

    nb = [jnp.where(same_block(8), -a, 0.0).astype(BF16) for a in mats]
    n2 = [_dot(x, x) for x in nb]
    p = [eye + x.astype(F32) for x in nb]
    p = [x + _bdot(x, y) for x, y in zip(p, n2)]
    n4 = [_bdot(y, y) for y in n2]
    p = [x + _bdot(x, y) for x, y in zip(p, n4)]
    s = 8
    while s < n:
        pick = jnp.logical_and(same_block(2 * s), jnp.logical_not(same_block(s)))
        off = [jnp.where(pick, a, 0.0).astype(BF16) for a in mats]
        pb = [x.astype(BF16) for x in p]
        y = [_dot(o, x) for o, x in zip(off, pb)]
        p = [x - _bdot(xb, yy) for x, xb, yy in zip(p, pb, y)]
        s *= 2
    return p


def _gdn_kernel(x_ref, gt_ref, cw_ref, alog_ref, dtb_ref, og_ref, o_ref, state_ref, carry_ref):
    t = pl.program_id(1)
    c = CHUNK
    nh = GDN_HEADS
    dk = LANE
    qk_w = nh * dk

    @pl.when(t == 0)
    def _():
        state_ref[...] = jnp.zeros_like(state_ref)
        carry_ref[...] = jnp.zeros_like(carry_ref)

    row = lax.broadcasted_iota(jnp.int32, (c, c), 0)
    col = lax.broadcasted_iota(jnp.int32, (c, c), 1)
    causal = col <= row
    strict = col < row
    tril = causal.astype(F32)
    triu = (row <= col).astype(F32)
    heads = range(nh)
    state = [state_ref[h] for h in heads]
    carry = carry_ref[...]
    for r0 in range(0, x_ref.shape[1], c):
        state, carry = _gdn_chunk(x_ref, gt_ref, cw_ref, alog_ref, dtb_ref, og_ref, o_ref, r0,
                                  _keep_rows(t, c) if r0 == 0 else None, state, carry,
                                  row, col, causal, strict, tril, triu)
    for h in heads:
        state_ref[h] = state[h]
    carry_ref[...] = carry


def _gdn_chunk(x_ref, gt_ref, cw_ref, alog_ref, dtb_ref, og_ref, o_ref, r0, keep, state, carry,
               row, col, causal, strict, tril, triu):
    c = CHUNK
    nh = GDN_HEADS
    dk = LANE
    qk_w = nh * dk
    heads = range(nh)

    x = x_ref[0, r0:r0 + c, 0:3 * qk_w].astype(F32)
    ext = jnp.concatenate([carry, x], axis=0)
    cw = cw_ref[...]
    conv = (cw[3:4] * x + cw[2:3] * ext[7:7 + c] + cw[1:2] * ext[6:6 + c] + cw[0:1] * ext[5:5 + c])
    carry = x[c - 8:c]
    qkv = conv * jax.nn.sigmoid(conv)

    gts = gt_ref[0, r0:r0 + c, :]
    beta_all = jax.nn.sigmoid(gts)
    g_all = -jnp.exp(alog_ref[...]) * jax.nn.softplus(gts + dtb_ref[...])
    if keep is not None:
        beta_all = jnp.where(keep, beta_all, 0.0)
        g_all = jnp.where(keep, g_all, 0.0)

    gc_all = jnp.dot(tril, g_all, preferred_element_type=F32, precision=lax.Precision.HIGHEST)
    gct_all = jnp.dot(g_all.T, triu, preferred_element_type=F32, precision=lax.Precision.HIGHEST)

    qn, kn, vb, kb, decay, egc, gcol = [], [], [], [], [], [], []
    for h in heads:
        qh = qkv[:, h * dk:(h + 1) * dk]
        kh = qkv[:, qk_w + h * dk:qk_w + (h + 1) * dk]
        vh = qkv[:, 2 * qk_w + h * dk:2 * qk_w + (h + 1) * dk]
        qn.append(qh * lax.rsqrt(jnp.sum(qh * qh, axis=-1, keepdims=True) + EPS) * (dk ** -0.5))
        kn.append(kh * lax.rsqrt(jnp.sum(kh * kh, axis=-1, keepdims=True) + EPS))
        beta = beta_all[:, h:h + 1]
        gcol.append(gc_all[:, nh + h:nh + h + 1])
        grow = gct_all[nh + h:nh + h + 1, :]
        decay.append(jnp.where(causal, jnp.exp(gcol[h] - grow), 0.0))
        egc.append(jnp.exp(gcol[h]))
        kb.append(kn[h] * beta)
        vb.append(vh * beta)
    kq = [_dot_nt(jnp.concatenate([kb[h], qn[h]], axis=0).astype(BF16), kn[h].astype(BF16))
          for h in heads]
    tinv = _tri_inverse([jnp.where(strict, kq[h][:c] * decay[h], 0.0) for h in heads], row, col)
    uw = [_bdot(tinv[h], jnp.concatenate([vb[h], kb[h] * egc[h]], axis=1)) for h in heads]
    ws_qs = [_bdot(jnp.concatenate([uw[h][:, dk:], qn[h] * egc[h]], axis=0), state[h]) for h in heads]
    v_new = [uw[h][:, :dk] - ws_qs[h][:c] for h in heads]
    o = [ws_qs[h][c:] + _bdot(kq[h][c:] * decay[h], v_new[h]) for h in heads]
    g_last = [gcol[h][c - 1:c] for h in heads]
    kv = [_dot_tn((kn[h] * jnp.exp(g_last[h] - gcol[h])).astype(BF16), v_new[h].astype(BF16)) for h in heads]
    for h in heads:
        z = x_ref[0, r0:r0 + c, 3 * qk_w + h * dk:3 * qk_w + (h + 1) * dk].astype(F32)
        on = _rms(o[h], og_ref[...])
        o_ref[0, r0:r0 + c, h * dk:(h + 1) * dk] = (on * (z * jax.nn.sigmoid(z))).astype(o_ref.dtype)
    return [state[h] * jnp.exp(g_last[h]) + kv[h] for h in heads], carry


def _gdn_core(proj, gates, conv_w, alog_row, dtb_row, o_gain):
    b, lp, width = proj.shape
    v_w = GDN_HEADS * LANE
    c = ROW_TILE
    return pl.pallas_call(
        _gdn_kernel,
        grid=(b, lp // c),
        in_specs=[pl.BlockSpec((1, c, width), lambda i, t: (i, t, 0)),
                  pl.BlockSpec((1, c, LANE), lambda i, t: (i, t, 0)),
                  _const_spec(conv_w.shape), _const_spec((1, LANE)), _const_spec((1, LANE)),
                  _const_spec((1, LANE))],
        out_specs=pl.BlockSpec((1, c, v_w), lambda i, t: (i, t, 0)),
        out_shape=jax.ShapeDtypeStruct((b, lp, v_w), BF16),
        scratch_shapes=[pltpu.VMEM((GDN_HEADS, LANE, LANE), F32),
                        pltpu.VMEM((8, 3 * v_w), F32)],
        compiler_params=_cparams(2),
        name="gdn_core",
    )(proj, gates, conv_w, alog_row, dtb_row, o_gain.reshape(1, LANE))


def _diff_attn_kernel(q_ref, k_ref, vt_ref, bias_ref, lq1_ref, lk1_ref, lq2_ref, lk2_ref, sub_ref, o_in_ref,
                      o_ref, qq_ref, *chunk_refs, tile, lam_init):
    del o_in_ref
    tq = q_ref.shape[1]
    tk = KV_TILE
    kv_len = k_ref.shape[1]
    dh = LANE // 2
    qc = Q_CHUNK
    n_chunks = 2 * tq // qc
    s_refs, m_refs, acc_refs = (chunk_refs[n::3] for n in range(3))
    q_lo = tile * tq
    n_kv = -(-(q_lo + tq) // tk)

    q = q_ref[0]
    lane = lax.broadcasted_iota(jnp.int32, (1, LANE), 1)
    zero = jnp.zeros_like(q)
    qq_ref[0:tq, :] = jnp.where(lane < dh, q, zero)
    qq_ref[tq:2 * tq, :] = jnp.where(lane >= dh, q, zero)

    bias = bias_ref[0]
    step_bias = bias[1:2, 0:1] * float(tk)
    krel = lax.broadcasted_iota(jnp.int32, (tk, qc), 0)
    ones_rows = jnp.ones((SUM_ROWS, tk), BF16)
    c2 = lax.broadcasted_iota(jnp.int32, (1, 2 * tq), 1)
    qpos = q_lo + jnp.where(c2 >= tq, c2 - tq, c2)

    def block_start(j):
        return min(j * tk, kv_len - tk)

    def scores(j, ci):
        start = block_start(j)
        return _dot_nt(k_ref[0, start:start + tk, :], qq_ref[ci * qc:(ci + 1) * qc, :])

    for ci in range(n_chunks):
        s_refs[ci][...] = scores(0, ci)
        m_refs[ci][...] = jnp.full_like(m_refs[ci], NEG)
        acc_refs[ci][...] = jnp.zeros_like(acc_refs[ci])

    for j in range(n_kv):
        start = block_start(j)
        diagonal = j * tk + tk - 1 > q_lo or start != j * tk
        vt = jnp.concatenate([vt_ref[0, :, start:start + tk], ones_rows], axis=0)
        c = step_bias * (start / tk)
        for ci in range(n_chunks):
            s = s_refs[ci][...] + bias
            if j + 1 < n_kv:
                s_next = scores(j + 1, ci)
            if diagonal:
                kpos = start + krel
                valid = jnp.logical_and(kpos <= qpos[:, ci * qc:(ci + 1) * qc], kpos >= max(j * tk, PADF))
                s = jnp.where(valid, s, NEG)
            elif j == 0:
                s = jnp.where(krel >= PADF, s, NEG)
            m_prev = m_refs[ci][...]
            m_new = jnp.maximum(m_prev, jnp.max(s, axis=0, keepdims=True) + c)
            alpha = jnp.exp2(m_prev - m_new)
            p = jnp.exp2((s - (m_new - c)).astype(BF16))
            acc_refs[ci][...] = alpha * acc_refs[ci][...] + _dot(vt, p)
            m_refs[ci][...] = m_new
            if j + 1 < n_kv:
                s_refs[ci][...] = s_next

    accs = [acc_refs[ci][...] for ci in range(n_chunks)]
    o = jnp.concatenate([a[0:LANE] / a[LANE:LANE + 1] for a in accs], axis=1)
    lam = (jnp.exp(jnp.sum(lq1_ref[...] * lk1_ref[...], axis=-1, keepdims=True))
           - jnp.exp(jnp.sum(lq2_ref[...] * lk2_ref[...], axis=-1, keepdims=True)) + lam_init)
    od = o[:, :tq] - lam * o[:, tq:]
    on = od * lax.rsqrt(jnp.mean(od * od, axis=0, keepdims=True) + EPS) * sub_ref[...]
    o_ref[0] = (on * (1.0 - lam_init)).T.astype(o_ref.dtype)


def _diff_attn(q, k, vt, bias, lq1, lk1, lq2, lk2, subln, lam_init):
    b, lp, width = q.shape
    nh = DIFF_HEADS
    tq = ROW_TILE
    tk = KV_TILE
    dh = lq1.shape[-1]
    vec = lambda a: a.reshape(1, dh).astype(F32)
    chunk_scratch = [pltpu.VMEM((tk, Q_CHUNK), F32), pltpu.VMEM((1, Q_CHUNK), F32),
                     pltpu.VMEM((LANE + SUM_ROWS, Q_CHUNK), F32)]
    out = jnp.zeros((b, lp, width), BF16)
    for tile in range(lp // tq):
        kv_len = min(-(-(tile + 1) * tq // tk) * tk, lp)
        out = pl.pallas_call(
            functools.partial(_diff_attn_kernel, tile=tile, lam_init=lam_init),
            grid=(b, nh),
            in_specs=[pl.BlockSpec((1, tq, LANE), lambda bi, h, tile=tile: (bi, tile, h)),
                      pl.BlockSpec((1, kv_len, LANE), lambda bi, h: (bi, 0, h)),
                      pl.BlockSpec((1, LANE, kv_len), lambda bi, h: (bi, h, 0)),
                      pl.BlockSpec((1, tk, Q_CHUNK), lambda bi, h: (h, 0, 0)),
                      _const_spec((1, dh)), _const_spec((1, dh)), _const_spec((1, dh)), _const_spec((1, dh)),
                      _const_spec((LANE, 1)),
                      pl.BlockSpec(memory_space=pl.ANY)],
            out_specs=pl.BlockSpec((1, tq, LANE), lambda bi, h, tile=tile: (bi, tile, h)),
            out_shape=jax.ShapeDtypeStruct((b, lp, width), BF16),
            scratch_shapes=[pltpu.VMEM((2 * tq, LANE), BF16)] + chunk_scratch * (2 * tq // Q_CHUNK),
            input_output_aliases={9: 0},
            compiler_params=_cparams(2),
            name="diff_attn",
        )(q, k, vt, bias, vec(lq1), vec(lk1), vec(lq2), vec(lk2), subln.reshape(LANE, 1).astype(F32), out)
    return out


def _ffn_kernel(h_ref, y_ref, wo_ref, g_ref, wg_ref, wu_ref, cg_ref, cu_ref, wd_ref, fg_ref, o_ref,
                xn_ref, acc_ref, carry_g_ref, carry_u_ref, ug_a, uu_a, ug_b, uu_b, *, final):
    t = pl.program_id(1)
    tm = h_ref.shape[1]
    nf = wg_ref.shape[0]
    slot_a = (ug_a, uu_a)
    slot_b = (ug_b, uu_b)

    @pl.when(t == 0)
    def _():
        carry_g_ref[...] = jnp.zeros_like(carry_g_ref)
        carry_u_ref[...] = jnp.zeros_like(carry_u_ref)

    x = jnp.where(_keep_rows(t, tm), h_ref[0] + _dot(y_ref[0], wo_ref[...]), 0.0)
    xn_ref[...] = _rms(x, g_ref[...]).astype(BF16)
    acc_ref[...] = x

    rc = LANE

    def up_gate(f, slot):
        slot[0][8:8 + tm, :] = _dot(xn_ref[...], wg_ref[f])

    def up_up(f, slot):
        slot[1][8:8 + tm, :] = _dot(xn_ref[...], wu_ref[f])

    def load_halo(f, slot):
        for u_ref, carry_ref in ((slot[0], carry_g_ref), (slot[1], carry_u_ref)):
            u_ref[0:8, :] = carry_ref[f]
            carry_ref[f] = u_ref[tm:tm + 8, :]

    def conv(u_ref, cw, r0):
        return (cw[2:3] * u_ref[8 + r0:8 + r0 + rc, :] + cw[1:2] * u_ref[7 + r0:7 + r0 + rc, :]
                + cw[0:1] * u_ref[6 + r0:6 + r0 + rc, :])

    def act_down(f, slot, r0):
        gate = conv(slot[0], cg_ref[f], r0)
        up = conv(slot[1], cu_ref[f], r0)
        act = (gate * jax.nn.sigmoid(gate) * up).astype(BF16)
        acc_ref[r0:r0 + rc, :] += _dot(act, wd_ref[f])

    def stage(f, cur, other, last=False):
        load_halo(f, cur)
        if not last:
            up_gate(f + 1, other)
        act_down(f, cur, 0)
        if not last:
            up_up(f + 1, other)
        for r0 in range(rc, tm, rc):
            act_down(f, cur, r0)

    up_gate(0, slot_a)
    up_up(0, slot_a)
    for f in range(0, nf - 1, 2):
        stage(f, slot_a, slot_b)
        stage(f + 1, slot_b, slot_a)
    stage(nf - 1, slot_a, slot_b, last=True)
    out = acc_ref[...]
    if final:
        out = _rms(out, fg_ref[...])
    o_ref[0] = jnp.where(_keep_rows(t, tm), out, 0.0)


def _ffn(h, y, wo, g, wg, wu, cg, cu, wd, final_gain=None):
    b, lp, d = h.shape
    tm = ROW_TILE
    nf, _, tf = wg.shape
    final = final_gain is not None
    fg = (final_gain if final else g).reshape(1, d)
    return pl.pallas_call(
        functools.partial(_ffn_kernel, final=final),
        grid=(b, lp // tm),
        in_specs=[pl.BlockSpec((1, tm, d), lambda i, t: (i, t, 0)),
                  pl.BlockSpec((1, tm, y.shape[2]), lambda i, t: (i, t, 0)),
                  _const_spec(wo.shape), _const_spec((1, d)),
                  _const_spec(wg.shape), _const_spec(wu.shape), _const_spec(cg.shape),
                  _const_spec(cu.shape), _const_spec(wd.shape), _const_spec((1, d))],
        out_specs=pl.BlockSpec((1, tm, d), lambda i, t: (i, t, 0)),
        out_shape=jax.ShapeDtypeStruct(h.shape, h.dtype),
        scratch_shapes=[pltpu.VMEM((tm, d), BF16), pltpu.VMEM((tm, d), F32),
                        pltpu.VMEM((nf, 8, tf), F32), pltpu.VMEM((nf, 8, tf), F32)]
        + [pltpu.VMEM((tm + 8, tf), F32)] * 4,
        input_output_aliases={0: 0},
        compiler_params=_cparams(2),
        name="conv_ffn",
    )(h, y, wo, g.reshape(1, d), wg, wu, cg, cu, wd, fg)


def _lane_row(vals, offset):
    return jnp.zeros((1, LANE), F32).at[0, offset:offset + vals.shape[0]].set(vals.astype(F32))


def kernel(x, meta_tokens, a_norm, a_w_in, a_conv, a_log, a_dt_bias, a_onorm, a_w_o, kv_norm, w_kv, lambda_k1, lambda_k2, b_norm, b_w_q, b_lambda_q1, b_lambda_q2, b_subln, b_w_o, ffn_norm, ffn_w_up, ffn_conv, ffn_w_down, final_norm):
    bn, seq, d = x.shape
    n_a = a_w_in.shape[0]
    n_b = b_w_q.shape[0]
    depth = n_a + n_b
    nh = GDN_HEADS
    qkvz_w = 4 * nh * LANE
    assert d == nh * LANE and seq % LANE == 0 and (LANE + seq) % ROW_TILE == 0
    d_ff = ffn_w_down.shape[1]
    assert d_ff % FF_TILE == 0
    nf = d_ff // FF_TILE
    assert nf >= 3 and nf % 2 == 1

    meta = jnp.broadcast_to(meta_tokens.astype(x.dtype)[None], (bn, N_META, d))
    h = jnp.concatenate([jnp.zeros((bn, PADF, d), x.dtype), meta, x], axis=1)

    def ffn_layer(h, mixer_out, w_o, layer):
        w_up = ffn_w_up[layer].astype(BF16).reshape(d, 2, nf, FF_TILE).transpose(1, 2, 0, 3)
        cw = ffn_conv[layer].astype(F32).reshape(FFN_CONV, 2, nf, FF_TILE).transpose(1, 2, 0, 3)
        w_dn = ffn_w_down[layer].astype(BF16).reshape(nf, FF_TILE, d)
        return _ffn(h, mixer_out, w_o.astype(BF16), ffn_norm[layer], w_up[0], w_up[1], cw[0], cw[1], w_dn,
                    final_gain=final_norm if layer == depth - 1 else None)

    for layer in range(n_a):
        w_in = a_w_in[layer]
        w_main = w_in[:, :qkvz_w].astype(BF16)
        w_gate = jnp.pad(w_in[:, qkvz_w:], ((0, 0), (0, LANE - 2 * nh))).astype(BF16)
        proj, gates = _norm_matmul(h, a_norm[layer], [w_main, w_gate], [BF16, F32], [1.0, 1.0])
        o = _gdn_core(proj, gates, a_conv[layer].astype(F32), _lane_row(a_log[layer], nh),
                      _lane_row(a_dt_bias[layer], nh), a_onorm[layer].astype(F32))
        h = ffn_layer(h, o, a_w_o[layer], layer)

    diff_w = DIFF_HEADS * LANE
    k_sh, vt_sh = _norm_matmul(h, kv_norm, [w_kv[:, :diff_w].astype(BF16), w_kv[:, diff_w:].astype(BF16)],
                               [BF16, BF16], [1.0, 1.0], transposed=[False, True])
    slopes = jnp.exp2(-(8.0 / DIFF_HEADS) * jnp.arange(1, DIFF_HEADS + 1, dtype=F32))
    bias_tab = (slopes * LOG2E)[:, None, None] * jnp.arange(KV_TILE, dtype=F32)[None, :, None]
    bias_tab = jnp.broadcast_to(bias_tab, (DIFF_HEADS, KV_TILE, Q_CHUNK))
    dh = lambda_k1.shape[0]
    for j in range(n_b):
        layer = n_a + j
        lam_init = 0.8 - 0.6 * math.exp(-0.3 * layer)
        (q,) = _norm_matmul(h, b_norm[j], [b_w_q[j].astype(BF16)], [BF16], [dh ** -0.5 * LOG2E])
        o = _diff_attn(q, k_sh, vt_sh, bias_tab, b_lambda_q1[j], lambda_k1, b_lambda_q2[j], lambda_k2,
                       b_subln[j], lam_init)
        h = ffn_layer(h, o, b_w_o[j], layer)

    return h[:, LANE:]
```

```python
import functools
import math

import jax
import jax.numpy as jnp
from jax import lax
from jax.experimental import pallas as pl
from jax.experimental.pallas import tpu as pltpu

N_META = 16
GDN_HEADS = 8
GDN_CONV = 4
DIFF_HEADS = 8
FFN_CONV = 3
EPS = 1e-6

LANE = 128
PADF = LANE - N_META
CHUNK = 128
ROW_TILE = 384
KV_TILE = 512
Q_CHUNK = 256
FF_TILE = 256
SUM_ROWS = 16
LOG2E = math.log2(math.e)
NEG = -1e30
VMEM_LIMIT = 56 * 1024 * 1024

F32 = jnp.float32
BF16 = jnp.bfloat16


def _cparams(n_axes):
    return pltpu.CompilerParams(
        dimension_semantics=("arbitrary",) * n_axes,
        vmem_limit_bytes=VMEM_LIMIT,
    )


def _const_spec(shape):
    zeros = (0,) * len(shape)
    return pl.BlockSpec(shape, lambda *_: zeros, pipeline_mode=pl.Buffered(1))


def _rms(x, g):
    return x * lax.rsqrt(jnp.mean(x * x, axis=-1, keepdims=True) + EPS) * g


def _keep_rows(t, rows):
    r = lax.broadcasted_iota(jnp.int32, (rows, 1), 0)
    return jnp.logical_or(t > 0, r >= PADF)


def _dot(a, b):
    return jnp.dot(a, b, preferred_element_type=F32)


def _dot_nt(a, b):
    return lax.dot_general(a, b, (((1,), (1,)), ((), ())), preferred_element_type=F32)


def _dot_tn(a, b):
    return lax.dot_general(a, b, (((0,), (0,)), ((), ())), preferred_element_type=F32)


def _norm_matmul_kernel(*refs, n_out, scales, transposed, n_chunk):
    h_ref, g_ref = refs[0], refs[1]
    w_refs = refs[2:2 + n_out]
    o_refs = refs[2 + n_out:2 + 2 * n_out]
    xn = _rms(h_ref[0], g_ref[...]).astype(BF16)
    for w_ref, o_ref, scale, tr in zip(w_refs, o_refs, scales, transposed):
        n = w_ref.shape[-1]
        step = min(n_chunk, n)
        for n0 in range(0, n, step):
            acc = _dot(xn, w_ref[:, n0:n0 + step])
            if scale != 1.0:
                acc = acc * scale
            if tr:
                o_ref[0, n0:n0 + step, :] = acc.T.astype(o_ref.dtype)
            else:
                o_ref[0, :, n0:n0 + step] = acc.astype(o_ref.dtype)


def _norm_matmul(h, g, ws, out_dtypes, scales, transposed=None):
    b, lp, d = h.shape
    tm = ROW_TILE
    transposed = tuple(transposed) if transposed is not None else (False,) * len(ws)
    kern = functools.partial(_norm_matmul_kernel, n_out=len(ws), scales=tuple(scales),
                             transposed=transposed, n_chunk=512)
    out_specs, out_shape = [], []
    for w, dt, tr in zip(ws, out_dtypes, transposed):
        n = w.shape[1]
        if tr:
            out_specs.append(pl.BlockSpec((1, n, tm), lambda i, t: (i, 0, t)))
            out_shape.append(jax.ShapeDtypeStruct((b, n, lp), dt))
        else:
            out_specs.append(pl.BlockSpec((1, tm, n), lambda i, t: (i, t, 0)))
            out_shape.append(jax.ShapeDtypeStruct((b, lp, n), dt))
    return pl.pallas_call(
        kern,
        grid=(b, lp // tm),
        in_specs=[pl.BlockSpec((1, tm, d), lambda i, t: (i, t, 0)), _const_spec((1, d))]
        + [_const_spec(w.shape) for w in ws],
        out_specs=out_specs,
        out_shape=out_shape,
        compiler_params=_cparams(2),
        name="norm_matmul",
    )(h, g.reshape(1, d), *ws)


def _bdot(a, b):
    return _dot(a.astype(BF16), b.astype(BF16))


def _tri_inverse(mats, row, col):
    n = mats[0].shape[0]
    eye = (row == col).astype(F32)

    def same_block(s):
        sh = s.bit_length() - 1
        return lax.shift_right_logical(row, sh) == lax.shift_right_logical(col, sh)

    nb = [jnp.where(same_block(8), -a, 0.0).astype(BF16) for a in mats]
    n2 = [_dot(x, x) for x in nb]
    p = [eye + x.astype(F32) for x in nb]
    p = [x + _bdot(x, y) for x, y in zip(p, n2)]
    n4 = [_bdot(y, y) for y in n2]
    p = [x + _bdot(x, y) for x, y in zip(p, n4)]
    s = 8
    while s < n:
        pick = jnp.logical_and(same_block(2 * s), jnp.logical_not(same_block(s)))
        off = [jnp.where(pick, a, 0.0).astype(BF16) for a in mats]
        pb = [x.astype(BF16) for x in p]
        y = [_dot(o, x) for o, x in zip(off, pb)]
        p = [x - _bdot(xb, yy) for x, xb, yy in zip(p, pb, y)]
        s *= 2
    return p


def _gdn_kernel(x_ref, gt_ref, cw_ref, shift_ref, alog_ref, dtb_ref, og_ref, o_ref, state_ref, prev_ref):
    t = pl.program_id(1)
    c = CHUNK
    nh = GDN_HEADS
    dk = LANE
    qk_w = nh * dk
    rows = x_ref.shape[1]

    @pl.when(t == 0)
    def _():
        state_ref[...] = jnp.zeros_like(state_ref)
        prev_ref[...] = jnp.zeros_like(prev_ref)

    row = lax.broadcasted_iota(jnp.int32, (c, c), 0)
    col = lax.broadcasted_iota(jnp.int32, (c, c), 1)
    causal = col <= row
    strict = col < row
    tril = causal.astype(F32)
    triu = (row <= col).astype(F32)
    heads = range(nh)
    state = [state_ref[h] for h in heads]
    for r0 in range(0, rows, c):
        prev = prev_ref[...] if r0 == 0 else x_ref[0, r0 - c:r0, 0:3 * qk_w]
        state = _gdn_chunk(x_ref, gt_ref, cw_ref, shift_ref, alog_ref, dtb_ref, og_ref, o_ref, r0,
                           _keep_rows(t, c) if r0 == 0 else None, state, prev,
                           row, col, causal, strict, tril, triu)
    for h in heads:
        state_ref[h] = state[h]
    prev_ref[...] = x_ref[0, rows - c:rows, 0:3 * qk_w]


def _gdn_chunk(x_ref, gt_ref, cw_ref, shift_ref, alog_ref, dtb_ref, og_ref, o_ref, r0, keep, state, prev,
               row, col, causal, strict, tril, triu):
    c = CHUNK
    nh = GDN_HEADS
    dk = LANE
    qk_w = nh * dk
    heads = range(nh)

    cw = cw_ref[...]
    group = 2 * LANE
    qkv = []
    for g0 in range(0, 3 * qk_w, group):
        x = x_ref[0, r0:r0 + c, g0:g0 + group]
        sh = _dot(shift_ref[...], jnp.concatenate([prev[:, g0:g0 + group], x], axis=0))
        w = cw[:, g0:g0 + group]
        conv = (w[3:4] * x.astype(F32) + w[2:3] * sh[0:c] + w[1:2] * sh[c:2 * c] + w[0:1] * sh[2 * c:3 * c])
        qkv.append(conv * jax.nn.sigmoid(conv))
    qkv = jnp.concatenate(qkv, axis=1)

    gts = gt_ref[0, r0:r0 + c, :]
    beta_all = jax.nn.sigmoid(gts)
    g_all = -jnp.exp(alog_ref[...]) * jax.nn.softplus(gts + dtb_ref[...])
    if keep is not None:
        beta_all = jnp.where(keep, beta_all, 0.0)
        g_all = jnp.where(keep, g_all, 0.0)

    gc_all = jnp.dot(tril, g_all, preferred_element_type=F32, precision=lax.Precision.HIGHEST)
    gct_all = jnp.dot(g_all.T, triu, preferred_element_type=F32, precision=lax.Precision.HIGHEST)

    qn, kn, vb, kb, decay, egc, gcol = [], [], [], [], [], [], []
    for h in heads:
        qh = qkv[:, h * dk:(h + 1) * dk]
        kh = qkv[:, qk_w + h * dk:qk_w + (h + 1) * dk]
        vh = qkv[:, 2 * qk_w + h * dk:2 * qk_w + (h + 1) * dk]
        qn.append(qh * lax.rsqrt(jnp.sum(qh * qh, axis=-1, keepdims=True) + EPS) * (dk ** -0.5))
        kn.append(kh * lax.rsqrt(jnp.sum(kh * kh, axis=-1, keepdims=True) + EPS))
        beta = beta_all[:, h:h + 1]
        gcol.append(gc_all[:, nh + h:nh + h + 1])
        grow = gct_all[nh + h:nh + h + 1, :]
        decay.append(jnp.where(causal, jnp.exp(gcol[h] - grow), 0.0))
        egc.append(jnp.exp(gcol[h]))
        kb.append(kn[h] * beta)
        vb.append(vh * beta)
    kq = [_dot_nt(jnp.concatenate([kb[h], qn[h]], axis=0).astype(BF16), kn[h].astype(BF16))
          for h in heads]
    tinv = _tri_inverse([jnp.where(strict, kq[h][:c] * decay[h], 0.0) for h in heads], row, col)
    uw = [_bdot(tinv[h], jnp.concatenate([vb[h], kb[h] * egc[h]], axis=1)) for h in heads]
    ws_qs = [_bdot(jnp.concatenate([uw[h][:, dk:], qn[h] * egc[h]], axis=0), state[h]) for h in heads]
    v_new = [uw[h][:, :dk] - ws_qs[h][:c] for h in heads]
    o = [ws_qs[h][c:] + _bdot(kq[h][c:] * decay[h], v_new[h]) for h in heads]
    g_last = [gcol[h][c - 1:c] for h in heads]
    kv = [_dot_tn((kn[h] * jnp.exp(g_last[h] - gcol[h])).astype(BF16), v_new[h].astype(BF16)) for h in heads]
    for h in heads:
        z = x_ref[0, r0:r0 + c, 3 * qk_w + h * dk:3 * qk_w + (h + 1) * dk].astype(F32)
        on = _rms(o[h], og_ref[...])
        o_ref[0, r0:r0 + c, h * dk:(h + 1) * dk] = (on * (z * jax.nn.sigmoid(z))).astype(o_ref.dtype)
    return [state[h] * jnp.exp(g_last[h]) + kv[h] for h in heads]


def _gdn_core(proj, gates, conv_w, alog_row, dtb_row, o_gain):
    b, lp, width = proj.shape
    v_w = GDN_HEADS * LANE
    c = ROW_TILE
    r = jnp.arange((GDN_CONV - 1) * CHUNK)
    shift = (jnp.arange(2 * CHUNK)[None, :] == (CHUNK + r % CHUNK - (r // CHUNK + 1))[:, None]).astype(BF16)
    return pl.pallas_call(
        _gdn_kernel,
        grid=(b, lp // c),
        in_specs=[pl.BlockSpec((1, c, width), lambda i, t: (i, t, 0)),
                  pl.BlockSpec((1, c, LANE), lambda i, t: (i, t, 0)),
                  _const_spec(conv_w.shape), _const_spec(shift.shape), _const_spec((1, LANE)),
                  _const_spec((1, LANE)), _const_spec((1, LANE))],
        out_specs=pl.BlockSpec((1, c, v_w), lambda i, t: (i, t, 0)),
        out_shape=jax.ShapeDtypeStruct((b, lp, v_w), BF16),
        scratch_shapes=[pltpu.VMEM((GDN_HEADS, LANE, LANE), F32),
                        pltpu.VMEM((CHUNK, 3 * v_w), BF16)],
        compiler_params=_cparams(2),
        name="gdn_core",
    )(proj, gates, conv_w, shift, alog_row, dtb_row, o_gain.reshape(1, LANE))


def _diff_attn_kernel(q_ref, k_ref, vt_ref, bias_ref, lq1_ref, lk1_ref, lq2_ref, lk2_ref, sub_ref, *rest,
                      tile, lam_init):
    o_ref, qq_ref, *chunk_refs = rest[1:] if tile > 0 else rest
    tq = q_ref.shape[1]
    tk = KV_TILE
    kv_len = k_ref.shape[1]
    dh = LANE // 2
    qc = Q_CHUNK
    n_chunks = 2 * tq // qc
    s_refs, m_refs, acc_refs = (chunk_refs[n::3] for n in range(3))
    q_lo = tile * tq
    n_kv = -(-(q_lo + tq) // tk)

    q = q_ref[0]
    lane = lax.broadcasted_iota(jnp.int32, (1, LANE), 1)
    zero = jnp.zeros_like(q)
    qq_ref[0:tq, :] = jnp.where(lane < dh, q, zero)
    qq_ref[tq:2 * tq, :] = jnp.where(lane >= dh, q, zero)

    bias = bias_ref[0]
    step_bias = bias[1:2, 0:1] * float(tk)
    krel = lax.broadcasted_iota(jnp.int32, (tk, qc), 0)
    ones_rows = jnp.ones((SUM_ROWS, tk), BF16)
    c2 = lax.broadcasted_iota(jnp.int32, (1, 2 * tq), 1)
    qpos = q_lo + jnp.where(c2 >= tq, c2 - tq, c2)

    def block_start(j):
        return min(j * tk, kv_len - tk)

    def scores(j, ci):
        start = block_start(j)
        return _dot_nt(k_ref[0, start:start + tk, :], qq_ref[ci * qc:(ci + 1) * qc, :])

    for ci in range(n_chunks):
        s_refs[ci][...] = scores(0, ci)
        m_refs[ci][...] = jnp.full_like(m_refs[ci], NEG)
        acc_refs[ci][...] = jnp.zeros_like(acc_refs[ci])

    for j in range(n_kv):
        start = block_start(j)
        diagonal = j * tk + tk - 1 > q_lo or start != j * tk
        vt = jnp.concatenate([vt_ref[0, :, start:start + tk], ones_rows], axis=0)
        c = step_bias * (start / tk)
        for ci in range(n_chunks):
            s = s_refs[ci][...] + bias
            if j + 1 < n_kv:
                s_next = scores(j + 1, ci)
            if diagonal:
                kpos = start + krel
                valid = jnp.logical_and(kpos <= qpos[:, ci * qc:(ci + 1) * qc], kpos >= max(j * tk, PADF))
                s = jnp.where(valid, s, NEG)
            elif j == 0:
                s = jnp.where(krel >= PADF, s, NEG)
            m_prev = m_refs[ci][...]
            m_new = jnp.maximum(m_prev, jnp.max(s, axis=0, keepdims=True) + c)
            alpha = jnp.exp2(m_prev - m_new)
            p = jnp.exp2((s - (m_new - c)).astype(BF16))
            acc_refs[ci][...] = alpha * acc_refs[ci][...] + _dot(vt, p)
            m_refs[ci][...] = m_new
            if j + 1 < n_kv:
                s_refs[ci][...] = s_next

    accs = [acc_refs[ci][...] for ci in range(n_chunks)]
    o = jnp.concatenate([a[0:LANE] / a[LANE:LANE + 1] for a in accs], axis=1)
    lam = (jnp.exp(jnp.sum(lq1_ref[...] * lk1_ref[...], axis=-1, keepdims=True))
           - jnp.exp(jnp.sum(lq2_ref[...] * lk2_ref[...], axis=-1, keepdims=True)) + lam_init)
    od = o[:, :tq] - lam * o[:, tq:]
    on = od * lax.rsqrt(jnp.mean(od * od, axis=0, keepdims=True) + EPS) * sub_ref[...]
    o_ref[0] = (on * (1.0 - lam_init)).T.astype(o_ref.dtype)


def _diff_attn(q, k, vt, bias, lq1, lk1, lq2, lk2, subln, lam_init):
    b, lp, width = q.shape
    nh = DIFF_HEADS
    tq = ROW_TILE
    tk = KV_TILE
    dh = lq1.shape[-1]
    vec = lambda a: a.reshape(1, dh).astype(F32)
    chunk_scratch = [pltpu.VMEM((tk, Q_CHUNK), F32), pltpu.VMEM((1, Q_CHUNK), F32),
                     pltpu.VMEM((LANE + SUM_ROWS, Q_CHUNK), F32)]
    args = (q, k, vt, bias, vec(lq1), vec(lk1), vec(lq2), vec(lk2), subln.reshape(LANE, 1).astype(F32))
    out = None
    for tile in range(lp // tq):
        kv_len = min(-(-(tile + 1) * tq // tk) * tk, lp)
        in_specs = [pl.BlockSpec((1, tq, LANE), lambda bi, h, tile=tile: (bi, tile, h)),
                    pl.BlockSpec((1, kv_len, LANE), lambda bi, h: (bi, 0, h)),
                    pl.BlockSpec((1, LANE, kv_len), lambda bi, h: (bi, h, 0)),
                    pl.BlockSpec((1, tk, Q_CHUNK), lambda bi, h: (h, 0, 0)),
                    _const_spec((1, dh)), _const_spec((1, dh)), _const_spec((1, dh)), _const_spec((1, dh)),
                    _const_spec((LANE, 1))]
        prev = () if out is None else (out,)
        out = pl.pallas_call(
            functools.partial(_diff_attn_kernel, tile=tile, lam_init=lam_init),
            grid=(b, nh),
            in_specs=in_specs + [pl.BlockSpec(memory_space=pl.ANY)] * len(prev),
            out_specs=pl.BlockSpec((1, tq, LANE), lambda bi, h, tile=tile: (bi, tile, h)),
            out_shape=jax.ShapeDtypeStruct((b, lp, width), BF16),
            scratch_shapes=[pltpu.VMEM((2 * tq, LANE), BF16)] + chunk_scratch * (2 * tq // Q_CHUNK),
            input_output_aliases={len(args): 0} if prev else {},
            compiler_params=_cparams(2),
            name="diff_attn",
        )(*args, *prev)
    return out


def _ffn_kernel(h_ref, y_ref, wo_ref, g_ref, wg_ref, wu_ref, cg_ref, cu_ref, wd_ref, fg_ref, o_ref,
                xn_ref, acc_ref, carry_g_ref, carry_u_ref, ug_a, uu_a, ug_b, uu_b, *, final):
    t = pl.program_id(1)
    tm = h_ref.shape[1]
    nf = wg_ref.shape[0]
    slot_a = (ug_a, uu_a)
    slot_b = (ug_b, uu_b)

    @pl.when(t == 0)
    def _():
        carry_g_ref[...] = jnp.zeros_like(carry_g_ref)
        carry_u_ref[...] = jnp.zeros_like(carry_u_ref)

    x = jnp.where(_keep_rows(t, tm), h_ref[0] + _dot(y_ref[0], wo_ref[...]), 0.0)
    xn_ref[...] = _rms(x, g_ref[...]).astype(BF16)
    acc_ref[...] = x

    rc = LANE

    def up_gate(f, slot):
        slot[0][8:8 + tm, :] = _dot(xn_ref[...], wg_ref[f])

    def up_up(f, slot):
        slot[1][8:8 + tm, :] = _dot(xn_ref[...], wu_ref[f])

    def load_halo(f, slot):
        for u_ref, carry_ref in ((slot[0], carry_g_ref), (slot[1], carry_u_ref)):
            u_ref[0:8, :] = carry_ref[f]
            carry_ref[f] = u_ref[tm:tm + 8, :]

    def conv(u_ref, cw, r0):
        return (cw[2:3] * u_ref[8 + r0:8 + r0 + rc, :] + cw[1:2] * u_ref[7 + r0:7 + r0 + rc, :]
                + cw[0:1] * u_ref[6 + r0:6 + r0 + rc, :])

    def act_down(f, slot, r0):
        gate = conv(slot[0], cg_ref[f], r0)
        up = conv(slot[1], cu_ref[f], r0)
        act = (gate * jax.nn.sigmoid(gate) * up).astype(BF16)
        acc_ref[r0:r0 + rc, :] += _dot(act, wd_ref[f])

    def stage(f, cur, other, last=False):
        load_halo(f, cur)
        if not last:
            up_gate(f + 1, other)
        act_down(f, cur, 0)
        if not last:
            up_up(f + 1, other)
        for r0 in range(rc, tm, rc):
            act_down(f, cur, r0)

    up_gate(0, slot_a)
    up_up(0, slot_a)
    for f in range(0, nf - 1, 2):
        stage(f, slot_a, slot_b)
        stage(f + 1, slot_b, slot_a)
    stage(nf - 1, slot_a, slot_b, last=True)
    out = acc_ref[...]
    if final:
        out = _rms(out, fg_ref[...])
    o_ref[0] = jnp.where(_keep_rows(t, tm), out, 0.0)


def _ffn(h, y, wo, g, wg, wu, cg, cu, wd, final_gain=None):
    b, lp, d = h.shape
    tm = ROW_TILE
    nf, _, tf = wg.shape
    final = final_gain is not None
    fg = (final_gain if final else g).reshape(1, d)
    return pl.pallas_call(
        functools.partial(_ffn_kernel, final=final),
        grid=(b, lp // tm),
        in_specs=[pl.BlockSpec((1, tm, d), lambda i, t: (i, t, 0)),
                  pl.BlockSpec((1, tm, y.shape[2]), lambda i, t: (i, t, 0)),
                  _const_spec(wo.shape), _const_spec((1, d)),
                  _const_spec(wg.shape), _const_spec(wu.shape), _const_spec(cg.shape),
                  _const_spec(cu.shape), _const_spec(wd.shape), _const_spec((1, d))],
        out_specs=pl.BlockSpec((1, tm, d), lambda i, t: (i, t, 0)),
        out_shape=jax.ShapeDtypeStruct(h.shape, h.dtype),
        scratch_shapes=[pltpu.VMEM((tm, d), BF16), pltpu.VMEM((tm, d), F32),
                        pltpu.VMEM((nf, 8, tf), F32), pltpu.VMEM((nf, 8, tf), F32)]
        + [pltpu.VMEM((tm + 8, tf), F32)] * 4,
        input_output_aliases={0: 0},
        compiler_params=_cparams(2),
        name="conv_ffn",
    )(h, y, wo, g.reshape(1, d), wg, wu, cg, cu, wd, fg)


def _lane_row(vals, offset):
    return jnp.zeros((1, LANE), F32).at[0, offset:offset + vals.shape[0]].set(vals.astype(F32))


def kernel(x, meta_tokens, a_norm, a_w_in, a_conv, a_log, a_dt_bias, a_onorm, a_w_o, kv_norm, w_kv, lambda_k1, lambda_k2, b_norm, b_w_q, b_lambda_q1, b_lambda_q2, b_subln, b_w_o, ffn_norm, ffn_w_up, ffn_conv, ffn_w_down, final_norm):
    bn, seq, d = x.shape
    n_a = a_w_in.shape[0]
    n_b = b_w_q.shape[0]
    depth = n_a + n_b
    nh = GDN_HEADS
    qkvz_w = 4 * nh * LANE
    assert d == nh * LANE and seq % LANE == 0 and (LANE + seq) % ROW_TILE == 0
    d_ff = ffn_w_down.shape[1]
    assert d_ff % FF_TILE == 0
    nf = d_ff // FF_TILE
    assert nf >= 3 and nf % 2 == 1

    meta = jnp.broadcast_to(meta_tokens.astype(x.dtype)[None], (bn, N_META, d))
    h = jnp.concatenate([jnp.zeros((bn, PADF, d), x.dtype), meta, x], axis=1)

    def ffn_layer(h, mixer_out, w_o, layer):
        w_up = ffn_w_up[layer].astype(BF16).reshape(d, 2, nf, FF_TILE).transpose(1, 2, 0, 3)
        cw = ffn_conv[layer].astype(F32).reshape(FFN_CONV, 2, nf, FF_TILE).transpose(1, 2, 0, 3)
        w_dn = ffn_w_down[layer].astype(BF16).reshape(nf, FF_TILE, d)
        return _ffn(h, mixer_out, w_o.astype(BF16), ffn_norm[layer], w_up[0], w_up[1], cw[0], cw[1], w_dn,
                    final_gain=final_norm if layer == depth - 1 else None)

    for layer in range(n_a):
        w_in = a_w_in[layer]
        w_main = w_in[:, :qkvz_w].astype(BF16)
        w_gate = jnp.pad(w_in[:, qkvz_w:], ((0, 0), (0, LANE - 2 * nh))).astype(BF16)
        proj, gates = _norm_matmul(h, a_norm[layer], [w_main, w_gate], [BF16, F32], [1.0, 1.0])
        o = _gdn_core(proj, gates, a_conv[layer].astype(F32), _lane_row(a_log[layer], nh),
                      _lane_row(a_dt_bias[layer], nh), a_onorm[layer].astype(F32))
        h = ffn_layer(h, o, a_w_o[layer], layer)

    diff_w = DIFF_HEADS * LANE
    k_sh, vt_sh = _norm_matmul(h, kv_norm, [w_kv[:, :diff_w].astype(BF16), w_kv[:, diff_w:].astype(BF16)],
                               [BF16, BF16], [1.0, 1.0], transposed=[False, True])
    slopes = jnp.exp2(-(8.0 / DIFF_HEADS) * jnp.arange(1, DIFF_HEADS + 1, dtype=F32))
    bias_tab = (slopes * LOG2E)[:, None, None] * jnp.arange(KV_TILE, dtype=F32)[None, :, None]
    bias_tab = jnp.broadcast_to(bias_tab, (DIFF_HEADS, KV_TILE, Q_CHUNK))
    dh = lambda_k1.shape[0]
    for j in range(n_b):
        layer = n_a + j
        lam_init = 0.8 - 0.6 * math.exp(-0.3 * layer)
        (q,) = _norm_matmul(h, b_norm[j], [b_w_q[j].astype(BF16)], [BF16], [dh ** -0.5 * LOG2E])
        o = _diff_attn(q, k_sh, vt_sh, bias_tab, b_lambda_q1[j], lambda_k1, b_lambda_q2[j], lambda_k2,
                       b_subln[j], lam_init)
        h = ffn_layer(h, o, b_w_o[j], layer)

    return h[:, LANE:]
```

```python
import functools
import math

import jax
import jax.numpy as jnp
from jax import lax
from jax.experimental import pallas as pl
from jax.experimental.pallas import tpu as pltpu

N_META = 16
GDN_HEADS = 8
GDN_CONV = 4
DIFF_HEADS = 8
FFN_CONV = 3
EPS = 1e-6

LANE = 128
PADF = LANE - N_META
CHUNK = 128
ROW_TILE = 384
KV_TILE = 512
Q_CHUNK = 256
FF_TILE = 256
SUM_ROWS = 16
LOG2E = math.log2(math.e)
NEG = -1e30
VMEM_LIMIT = 56 * 1024 * 1024

F32 = jnp.float32
BF16 = jnp.bfloat16


def _cparams(n_axes):
    return pltpu.CompilerParams(
        dimension_semantics=("arbitrary",) * n_axes,
        vmem_limit_bytes=VMEM_LIMIT,
    )


def _const_spec(shape):
    zeros = (0,) * len(shape)
    return pl.BlockSpec(shape, lambda *_: zeros, pipeline_mode=pl.Buffered(1))


def _rms(x, g):
    return x * lax.rsqrt(jnp.mean(x * x, axis=-1, keepdims=True) + EPS) * g


def _keep_rows(t, rows):
    r = lax.broadcasted_iota(jnp.int32, (rows, 1), 0)
    return jnp.logical_or(t > 0, r >= PADF)


def _dot(a, b):
    return jnp.dot(a, b, preferred_element_type=F32)


def _dot_nt(a, b):
    return lax.dot_general(a, b, (((1,), (1,)), ((), ())), preferred_element_type=F32)


def _dot_tn(a, b):
    return lax.dot_general(a, b, (((0,), (0,)), ((), ())), preferred_element_type=F32)


def _norm_matmul_kernel(*refs, n_out, scales, transposed, n_chunk):
    h_ref, g_ref = refs[0], refs[1]
    w_refs = refs[2:2 + n_out]
    o_refs = refs[2 + n_out:2 + 2 * n_out]
    xn = _rms(h_ref[0], g_ref[...]).astype(BF16)
    for w_ref, o_ref, scale, tr in zip(w_refs, o_refs, scales, transposed):
        n = w_ref.shape[-1]
        step = min(n_chunk, n)
        for n0 in range(0, n, step):
            acc = _dot(xn, w_ref[:, n0:n0 + step])
            if scale != 1.0:
                acc = acc * scale
            if tr:
                o_ref[0, n0:n0 + step, :] = acc.T.astype(o_ref.dtype)
            else:
                o_ref[0, :, n0:n0 + step] = acc.astype(o_ref.dtype)


def _norm_matmul(h, g, ws, out_dtypes, scales, transposed=None):
    b, lp, d = h.shape
    tm = ROW_TILE
    transposed = tuple(transposed) if transposed is not None else (False,) * len(ws)
    kern = functools.partial(_norm_matmul_kernel, n_out=len(ws), scales=tuple(scales),
                             transposed=transposed, n_chunk=512)
    out_specs, out_shape = [], []
    for w, dt, tr in zip(ws, out_dtypes, transposed):
        n = w.shape[1]
        if tr:
            out_specs.append(pl.BlockSpec((1, n, tm), lambda i, t: (i, 0, t)))
            out_shape.append(jax.ShapeDtypeStruct((b, n, lp), dt))
        else:
            out_specs.append(pl.BlockSpec((1, tm, n), lambda i, t: (i, t, 0)))
            out_shape.append(jax.ShapeDtypeStruct((b, lp, n), dt))
    return pl.pallas_call(
        kern,
        grid=(b, lp // tm),
        in_specs=[pl.BlockSpec((1, tm, d), lambda i, t: (i, t, 0)), _const_spec((1, d))]
        + [_const_spec(w.shape) for w in ws],
        out_specs=out_specs,
        out_shape=out_shape,
        compiler_params=_cparams(2),
        name="norm_matmul",
    )(h, g.reshape(1, d), *ws)


def _gdn_in_proj_kernel(h_ref, g_ref, w_ref, wg_ref, cw_ref, o_ref, gate_ref, carry_ref, *, conv_w, n_chunk):
    t = pl.program_id(1)
    tm = h_ref.shape[1]

    @pl.when(t == 0)
    def _():
        carry_ref[...] = jnp.zeros_like(carry_ref)

    xn = _rms(h_ref[0], g_ref[...]).astype(BF16)
    gate_ref[0] = _dot(xn, wg_ref[...])
    gates = []
    for n0 in range(0, w_ref.shape[1], n_chunk):
        lhs = xn
        if len(gates) >= 2:
            late = jnp.max(gates[-2], axis=(0, 1), keepdims=True) > 2.0
            lhs = jnp.concatenate([jnp.where(late, jnp.zeros_like(xn[0:16]), xn[0:16]), xn[16:]], axis=0)
        acc = _dot(lhs, w_ref[:, n0:n0 + n_chunk])
        if n0 < conv_w:
            ext = jnp.concatenate([carry_ref[:, n0:n0 + n_chunk], acc], axis=0)
            carry_ref[:, n0:n0 + n_chunk] = acc[tm - 8:tm]
            cw = cw_ref[:, n0:n0 + n_chunk]
            acc = cw[3:4] * acc + cw[2:3] * ext[7:7 + tm] + cw[1:2] * ext[6:6 + tm] + cw[0:1] * ext[5:5 + tm]
            gates.append(jax.nn.sigmoid(acc))
            acc = acc * gates[-1]
        o_ref[0, :, n0:n0 + n_chunk] = acc.astype(o_ref.dtype)


def _gdn_in_proj(h, g, w_main, w_gate, conv_w):
    b, lp, d = h.shape
    tm = ROW_TILE
    n = w_main.shape[1]
    n_conv = conv_w.shape[1]
    n_chunk = 512
    assert n % n_chunk == 0 and n_conv % n_chunk == 0
    return pl.pallas_call(
        functools.partial(_gdn_in_proj_kernel, conv_w=n_conv, n_chunk=n_chunk),
        grid=(b, lp // tm),
        in_specs=[pl.BlockSpec((1, tm, d), lambda i, t: (i, t, 0)), _const_spec((1, d)),
                  _const_spec(w_main.shape), _const_spec(w_gate.shape), _const_spec(conv_w.shape)],
        out_specs=[pl.BlockSpec((1, tm, n), lambda i, t: (i, t, 0)),
                   pl.BlockSpec((1, tm, LANE), lambda i, t: (i, t, 0))],
        out_shape=[jax.ShapeDtypeStruct((b, lp, n), BF16), jax.ShapeDtypeStruct((b, lp, LANE), F32)],
        scratch_shapes=[pltpu.VMEM((8, n_conv), F32)],
        compiler_params=_cparams(2),
        name="gdn_in_proj",
    )(h, g.reshape(1, d), w_main, w_gate, conv_w)


def _bdot(a, b):
    return _dot(a.astype(BF16), b.astype(BF16))


def _tri_inverse(mats, row, col):
    n = mats[0].shape[0]
    eye = (row == col).astype(F32)

    def same_block(s):
        sh = s.bit_length() - 1
        return lax.shift_right_logical(row, sh) == lax.shift_right_logical(col, sh)

    nb = [jnp.where(same_block(8), -a, 0.0).astype(BF16) for a in mats]
    n2 = [_dot(x, x) for x in nb]
    p = [eye + x.astype(F32) for x in nb]
    p = [x + _bdot(x, y) for x, y in zip(p, n2)]
    n4 = [_bdot(y, y) for y in n2]
    p = [x + _bdot(x, y) for x, y in zip(p, n4)]
    s = 8
    while s < n:
        pick = jnp.logical_and(same_block(2 * s), jnp.logical_not(same_block(s)))
        off = [jnp.where(pick, a, 0.0).astype(BF16) for a in mats]
        pb = [x.astype(BF16) for x in p]
        y = [_dot(o, x) for o, x in zip(off, pb)]
        p = [x - _bdot(xb, yy) for x, xb, yy in zip(p, pb, y)]
        s *= 2
    return p


def _gdn_kernel(x_ref, gt_ref, alog_ref, dtb_ref, og_ref, o_ref, state_ref):
    t = pl.program_id(1)
    c = CHUNK
    nh = GDN_HEADS

    @pl.when(t == 0)
    def _():
        state_ref[...] = jnp.zeros_like(state_ref)

    row = lax.broadcasted_iota(jnp.int32, (c, c), 0)
    col = lax.broadcasted_iota(jnp.int32, (c, c), 1)
    causal = col <= row
    strict = col < row
    tril = causal.astype(F32)
    triu = (row <= col).astype(F32)
    heads = range(nh)
    state = [state_ref[h] for h in heads]
    for r0 in range(0, x_ref.shape[1], c):
        state = _gdn_chunk(x_ref, gt_ref, alog_ref, dtb_ref, og_ref, o_ref, r0,
                           _keep_rows(t, c) if r0 == 0 else None, state,
                           row, col, causal, strict, tril, triu)
    for h in heads:
        state_ref[h] = state[h]


def _gdn_chunk(x_ref, gt_ref, alog_ref, dtb_ref, og_ref, o_ref, r0, keep, state,
               row, col, causal, strict, tril, triu):
    c = CHUNK
    nh = GDN_HEADS
    dk = LANE
    qk_w = nh * dk
    heads = range(nh)

    qkv = x_ref[0, r0:r0 + c, 0:3 * qk_w].astype(F32)

    gts = gt_ref[0, r0:r0 + c, :]
    beta_all = jax.nn.sigmoid(gts)
    g_all = -jnp.exp(alog_ref[...]) * jax.nn.softplus(gts + dtb_ref[...])
    if keep is not None:
        beta_all = jnp.where(keep, beta_all, 0.0)
        g_all = jnp.where(keep, g_all, 0.0)

    gc_all = jnp.dot(tril, g_all, preferred_element_type=F32, precision=lax.Precision.HIGHEST)
    gct_all = jnp.dot(g_all.T, triu, preferred_element_type=F32, precision=lax.Precision.HIGHEST)

    qn, kn, vb, kb, decay, egc, gcol = [], [], [], [], [], [], []
    for h in heads:
        qh = qkv[:, h * dk:(h + 1) * dk]
        kh = qkv[:, qk_w + h * dk:qk_w + (h + 1) * dk]
        vh = qkv[:, 2 * qk_w + h * dk:2 * qk_w + (h + 1) * dk]
        qn.append(qh * lax.rsqrt(jnp.sum(qh * qh, axis=-1, keepdims=True) + EPS) * (dk ** -0.5))
        kn.append(kh * lax.rsqrt(jnp.sum(kh * kh, axis=-1, keepdims=True) + EPS))
        beta = beta_all[:, h:h + 1]
        gcol.append(gc_all[:, nh + h:nh + h + 1])
        grow = gct_all[nh + h:nh + h + 1, :]
        decay.append(jnp.where(causal, jnp.exp(gcol[h] - grow), 0.0))
        egc.append(jnp.exp(gcol[h]))
        kb.append(kn[h] * beta)
        vb.append(vh * beta)
    kq = [_dot_nt(jnp.concatenate([kb[h], qn[h]], axis=0).astype(BF16), kn[h].astype(BF16))
          for h in heads]
    tinv = _tri_inverse([jnp.where(strict, kq[h][:c] * decay[h], 0.0) for h in heads], row, col)
    uw = [_bdot(tinv[h], jnp.concatenate([vb[h], kb[h] * egc[h]], axis=1)) for h in heads]
    ws_qs = [_bdot(jnp.concatenate([uw[h][:, dk:], qn[h] * egc[h]], axis=0), state[h]) for h in heads]
    v_new = [uw[h][:, :dk] - ws_qs[h][:c] for h in heads]
    o = [ws_qs[h][c:] + _bdot(kq[h][c:] * decay[h], v_new[h]) for h in heads]
    g_last = [gcol[h][c - 1:c] for h in heads]
    kv = [_dot_tn((kn[h] * jnp.exp(g_last[h] - gcol[h])).astype(BF16), v_new[h].astype(BF16)) for h in heads]
    for h in heads:
        z = x_ref[0, r0:r0 + c, 3 * qk_w + h * dk:3 * qk_w + (h + 1) * dk].astype(F32)
        on = _rms(o[h], og_ref[...])
        o_ref[0, r0:r0 + c, h * dk:(h + 1) * dk] = (on * (z * jax.nn.sigmoid(z))).astype(o_ref.dtype)
    return [state[h] * jnp.exp(g_last[h]) + kv[h] for h in heads]


def _gdn_core(proj, gates, alog_row, dtb_row, o_gain):
    b, lp, width = proj.shape
    v_w = GDN_HEADS * LANE
    c = ROW_TILE
    return pl.pallas_call(
        _gdn_kernel,
        grid=(b, lp // c),
        in_specs=[pl.BlockSpec((1, c, width), lambda i, t: (i, t, 0)),
                  pl.BlockSpec((1, c, LANE), lambda i, t: (i, t, 0)),
                  _const_spec((1, LANE)), _const_spec((1, LANE)), _const_spec((1, LANE))],
        out_specs=pl.BlockSpec((1, c, v_w), lambda i, t: (i, t, 0)),
        out_shape=jax.ShapeDtypeStruct((b, lp, v_w), BF16),
        scratch_shapes=[pltpu.VMEM((GDN_HEADS, LANE, LANE), F32)],
        compiler_params=_cparams(2),
        name="gdn_core",
    )(proj, gates, alog_row, dtb_row, o_gain.reshape(1, LANE))


def _diff_attn_kernel(q_ref, k_ref, vt_ref, bias_ref, lq1_ref, lk1_ref, lq2_ref, lk2_ref, sub_ref, *rest,
                      tile, lam_init):
    o_ref, qq_ref, *chunk_refs = rest[1:] if tile > 0 else rest
    tq = q_ref.shape[1]
    tk = KV_TILE
    kv_len = k_ref.shape[1]
    dh = LANE // 2
    qc = Q_CHUNK
    n_chunks = 2 * tq // qc
    s_refs, m_refs, acc_refs = (chunk_refs[n::3] for n in range(3))
    q_lo = tile * tq
    n_kv = -(-(q_lo + tq) // tk)

    q = q_ref[0]
    lane = lax.broadcasted_iota(jnp.int32, (1, LANE), 1)
    zero = jnp.zeros_like(q)
    qq_ref[0:tq, :] = jnp.where(lane < dh, q, zero)
    qq_ref[tq:2 * tq, :] = jnp.where(lane >= dh, q, zero)

    bias = bias_ref[0]
    step_bias = bias[1:2, 0:1] * float(tk)
    krel = lax.broadcasted_iota(jnp.int32, (tk, qc), 0)
    ones_rows = jnp.ones((SUM_ROWS, tk), BF16)
    c2 = lax.broadcasted_iota(jnp.int32, (1, 2 * tq), 1)
    qpos = q_lo + jnp.where(c2 >= tq, c2 - tq, c2)

    def block_start(j):
        return min(j * tk, kv_len - tk)

    def scores(j, ci):
        start = block_start(j)
        return _dot_nt(k_ref[0, start:start + tk, :], qq_ref[ci * qc:(ci + 1) * qc, :])

    for ci in range(n_chunks):
        s_refs[ci][...] = scores(0, ci)
        m_refs[ci][...] = jnp.full_like(m_refs[ci], NEG)
        acc_refs[ci][...] = jnp.zeros_like(acc_refs[ci])

    for j in range(n_kv):
        start = block_start(j)
        diagonal = j * tk + tk - 1 > q_lo or start != j * tk
        vt = jnp.concatenate([vt_ref[0, :, start:start + tk], ones_rows], axis=0)
        c = step_bias * (start / tk)
        for ci in range(n_chunks):
            s = s_refs[ci][...] + bias
            if j + 1 < n_kv:
                s_next = scores(j + 1, ci)
            if diagonal:
                kpos = start + krel
                valid = jnp.logical_and(kpos <= qpos[:, ci * qc:(ci + 1) * qc], kpos >= max(j * tk, PADF))
                s = jnp.where(valid, s, NEG)
            elif j == 0:
                s = jnp.where(krel >= PADF, s, NEG)
            m_prev = m_refs[ci][...]
            m_new = jnp.maximum(m_prev, jnp.max(s, axis=0, keepdims=True) + c)
            alpha = jnp.exp2(m_prev - m_new)
            p = jnp.exp2((s - (m_new - c)).astype(BF16))
            acc_refs[ci][...] = alpha * acc_refs[ci][...] + _dot(vt, p)
            m_refs[ci][...] = m_new
            if j + 1 < n_kv:
                s_refs[ci][...] = s_next

    accs = [acc_refs[ci][...] for ci in range(n_chunks)]
    o = jnp.concatenate([a[0:LANE] / a[LANE:LANE + 1] for a in accs], axis=1)
    lam = (jnp.exp(jnp.sum(lq1_ref[...] * lk1_ref[...], axis=-1, keepdims=True))
           - jnp.exp(jnp.sum(lq2_ref[...] * lk2_ref[...], axis=-1, keepdims=True)) + lam_init)
    od = o[:, :tq] - lam * o[:, tq:]
    on = od * lax.rsqrt(jnp.mean(od * od, axis=0, keepdims=True) + EPS) * sub_ref[...]
    o_ref[0] = (on * (1.0 - lam_init)).T.astype(o_ref.dtype)


def _diff_attn(q, k, vt, bias, lq1, lk1, lq2, lk2, subln, lam_init):
    b, lp, width = q.shape
    nh = DIFF_HEADS
    tq = ROW_TILE
    tk = KV_TILE
    dh = lq1.shape[-1]
    vec = lambda a: a.reshape(1, dh).astype(F32)
    chunk_scratch = [pltpu.VMEM((tk, Q_CHUNK), F32), pltpu.VMEM((1, Q_CHUNK), F32),
                     pltpu.VMEM((LANE + SUM_ROWS, Q_CHUNK), F32)]
    args = (q, k, vt, bias, vec(lq1), vec(lk1), vec(lq2), vec(lk2), subln.reshape(LANE, 1).astype(F32))
    out = None
    for tile in range(lp // tq):
        kv_len = min(-(-(tile + 1) * tq // tk) * tk, lp)
        in_specs = [pl.BlockSpec((1, tq, LANE), lambda bi, h, tile=tile: (bi, tile, h)),
                    pl.BlockSpec((1, kv_len, LANE), lambda bi, h: (bi, 0, h)),
                    pl.BlockSpec((1, LANE, kv_len), lambda bi, h: (bi, h, 0)),
                    pl.BlockSpec((1, tk, Q_CHUNK), lambda bi, h: (h, 0, 0)),
                    _const_spec((1, dh)), _const_spec((1, dh)), _const_spec((1, dh)), _const_spec((1, dh)),
                    _const_spec((LANE, 1))]
        prev = () if out is None else (out,)
        out = pl.pallas_call(
            functools.partial(_diff_attn_kernel, tile=tile, lam_init=lam_init),
            grid=(b, nh),
            in_specs=in_specs + [pl.BlockSpec(memory_space=pl.ANY)] * len(prev),
            out_specs=pl.BlockSpec((1, tq, LANE), lambda bi, h, tile=tile: (bi, tile, h)),
            out_shape=jax.ShapeDtypeStruct((b, lp, width), BF16),
            scratch_shapes=[pltpu.VMEM((2 * tq, LANE), BF16)] + chunk_scratch * (2 * tq // Q_CHUNK),
            input_output_aliases={len(args): 0} if prev else {},
            compiler_params=_cparams(2),
            name="diff_attn",
        )(*args, *prev)
    return out


def _ffn_kernel(h_ref, y_ref, wo_ref, g_ref, wg_ref, wu_ref, cg_ref, cu_ref, wd_ref, fg_ref, o_ref,
                xn_ref, acc_ref, carry_g_ref, carry_u_ref, ug_a, uu_a, ug_b, uu_b, *, final):
    t = pl.program_id(1)
    tm = h_ref.shape[1]
    nf = wg_ref.shape[0]
    slot_a = (ug_a, uu_a)
    slot_b = (ug_b, uu_b)

    @pl.when(t == 0)
    def _():
        carry_g_ref[...] = jnp.zeros_like(carry_g_ref)
        carry_u_ref[...] = jnp.zeros_like(carry_u_ref)

    x = jnp.where(_keep_rows(t, tm), h_ref[0] + _dot(y_ref[0], wo_ref[...]), 0.0)
    xn_ref[...] = _rms(x, g_ref[...]).astype(BF16)
    acc_ref[...] = x

    rc = LANE

    def up_gate(f, slot):
        slot[0][8:8 + tm, :] = _dot(xn_ref[...], wg_ref[f])

    def up_up(f, slot):
        slot[1][8:8 + tm, :] = _dot(xn_ref[...], wu_ref[f])

    def load_halo(f, slot):
        for u_ref, carry_ref in ((slot[0], carry_g_ref), (slot[1], carry_u_ref)):
            u_ref[0:8, :] = carry_ref[f]
            carry_ref[f] = u_ref[tm:tm + 8, :]

    def conv(u_ref, cw, r0):
        return (cw[2:3] * u_ref[8 + r0:8 + r0 + rc, :] + cw[1:2] * u_ref[7 + r0:7 + r0 + rc, :]
                + cw[0:1] * u_ref[6 + r0:6 + r0 + rc, :])

    def act_down(f, slot, r0):
        gate = conv(slot[0], cg_ref[f], r0)
        up = conv(slot[1], cu_ref[f], r0)
        act = (gate * jax.nn.sigmoid(gate) * up).astype(BF16)
        acc_ref[r0:r0 + rc, :] += _dot(act, wd_ref[f])

    def stage(f, cur, other, last=False):
        load_halo(f, cur)
        if not last:
            up_gate(f + 1, other)
        act_down(f, cur, 0)
        if not last:
            up_up(f + 1, other)
        for r0 in range(rc, tm, rc):
            act_down(f, cur, r0)

    up_gate(0, slot_a)
    up_up(0, slot_a)
    for f in range(0, nf - 1, 2):
        stage(f, slot_a, slot_b)
        stage(f + 1, slot_b, slot_a)
    stage(nf - 1, slot_a, slot_b, last=True)
    out = acc_ref[...]
    if final:
        out = _rms(out, fg_ref[...])
    o_ref[0] = jnp.where(_keep_rows(t, tm), out, 0.0)


def _ffn(h, y, wo, g, wg, wu, cg, cu, wd, final_gain=None):
    b, lp, d = h.shape
    tm = ROW_TILE
    nf, _, tf = wg.shape
    final = final_gain is not None
    fg = (final_gain if final else g).reshape(1, d)
    return pl.pallas_call(
        functools.partial(_ffn_kernel, final=final),
        grid=(b, lp // tm),
        in_specs=[pl.BlockSpec((1, tm, d), lambda i, t: (i, t, 0)),
                  pl.BlockSpec((1, tm, y.shape[2]), lambda i, t: (i, t, 0)),
                  _const_spec(wo.shape), _const_spec((1, d)),
                  _const_spec(wg.shape), _const_spec(wu.shape), _const_spec(cg.shape),
                  _const_spec(cu.shape), _const_spec(wd.shape), _const_spec((1, d))],
        out_specs=pl.BlockSpec((1, tm, d), lambda i, t: (i, t, 0)),
        out_shape=jax.ShapeDtypeStruct(h.shape, h.dtype),
        scratch_shapes=[pltpu.VMEM((tm, d), BF16), pltpu.VMEM((tm, d), F32),
                        pltpu.VMEM((nf, 8, tf), F32), pltpu.VMEM((nf, 8, tf), F32)]
        + [pltpu.VMEM((tm + 8, tf), F32)] * 4,
        input_output_aliases={0: 0},
        compiler_params=_cparams(2),
        name="conv_ffn",
    )(h, y, wo, g.reshape(1, d), wg, wu, cg, cu, wd, fg)


def _lane_row(vals, offset):
    return jnp.zeros((1, LANE), F32).at[0, offset:offset + vals.shape[0]].set(vals.astype(F32))


def kernel(x, meta_tokens, a_norm, a_w_in, a_conv, a_log, a_dt_bias, a_onorm, a_w_o, kv_norm, w_kv, lambda_k1, lambda_k2, b_norm, b_w_q, b_lambda_q1, b_lambda_q2, b_subln, b_w_o, ffn_norm, ffn_w_up, ffn_conv, ffn_w_down, final_norm):
    bn, seq, d = x.shape
    n_a = a_w_in.shape[0]
    n_b = b_w_q.shape[0]
    depth = n_a + n_b
    nh = GDN_HEADS
    qkvz_w = 4 * nh * LANE
    assert d == nh * LANE and seq % LANE == 0 and (LANE + seq) % ROW_TILE == 0
    d_ff = ffn_w_down.shape[1]
    assert d_ff % FF_TILE == 0
    nf = d_ff // FF_TILE
    assert nf >= 3 and nf % 2 == 1

    meta = jnp.broadcast_to(meta_tokens.astype(x.dtype)[None], (bn, N_META, d))
    h = jnp.concatenate([jnp.zeros((bn, PADF, d), x.dtype), meta, x], axis=1)

    def ffn_layer(h, mixer_out, w_o, layer):
        w_up = ffn_w_up[layer].astype(BF16).reshape(d, 2, nf, FF_TILE).transpose(1, 2, 0, 3)
        cw = ffn_conv[layer].astype(F32).reshape(FFN_CONV, 2, nf, FF_TILE).transpose(1, 2, 0, 3)
        w_dn = ffn_w_down[layer].astype(BF16).reshape(nf, FF_TILE, d)
        return _ffn(h, mixer_out, w_o.astype(BF16), ffn_norm[layer], w_up[0], w_up[1], cw[0], cw[1], w_dn,
                    final_gain=final_norm if layer == depth - 1 else None)

    for layer in range(n_a):
        w_in = a_w_in[layer]
        w_main = w_in[:, :qkvz_w].astype(BF16)
        w_gate = jnp.pad(w_in[:, qkvz_w:], ((0, 0), (0, LANE - 2 * nh))).astype(BF16)
        proj, gates = _gdn_in_proj(h, a_norm[layer], w_main, w_gate, a_conv[layer].astype(F32))
        o = _gdn_core(proj, gates, _lane_row(a_log[layer], nh),
                      _lane_row(a_dt_bias[layer], nh), a_onorm[layer].astype(F32))
        h = ffn_layer(h, o, a_w_o[layer], layer)

    diff_w = DIFF_HEADS * LANE
    k_sh, vt_sh = _norm_matmul(h, kv_norm, [w_kv[:, :diff_w].astype(BF16), w_kv[:, diff_w:].astype(BF16)],
                               [BF16, BF16], [1.0, 1.0], transposed=[False, True])
    slopes = jnp.exp2(-(8.0 / DIFF_HEADS) * jnp.arange(1, DIFF_HEADS + 1, dtype=F32))
    bias_tab = (slopes * LOG2E)[:, None, None] * jnp.arange(KV_TILE, dtype=F32)[None, :, None]
    bias_tab = jnp.broadcast_to(bias_tab, (DIFF_HEADS, KV_TILE, Q_CHUNK))
    dh = lambda_k1.shape[0]
    for j in range(n_b):
        layer = n_a + j
        lam_init = 0.8 - 0.6 * math.exp(-0.3 * layer)
        (q,) = _norm_matmul(h, b_norm[j], [b_w_q[j].astype(BF16)], [BF16], [dh ** -0.5 * LOG2E])
        o = _diff_attn(q, k_sh, vt_sh, bias_tab, b_lambda_q1[j], lambda_k1, b_lambda_q2[j], lambda_k2,
                       b_subln[j], lam_init)
        h = ffn_layer(h, o, b_w_o[j], layer)

    return h[:, LANE:]
```

```python
import functools
import math

import jax
import jax.numpy as jnp
from jax import lax
from jax.experimental import pallas as pl
from jax.experimental.pallas import tpu as pltpu

N_META = 16
GDN_HEADS = 8
GDN_CONV = 4
DIFF_HEADS = 8
FFN_CONV = 3
EPS = 1e-6

LANE = 128
PADF = LANE - N_META
CHUNK = 128
ROW_TILE = 384
KV_TILE = 512
Q_CHUNK = 256
FF_TILE = 256
HEADS_PER_STEP = 2
SUM_ROWS = 16
LOG2E = math.log2(math.e)
NEG = -1e30
VMEM_LIMIT = 56 * 1024 * 1024

F32 = jnp.float32
BF16 = jnp.bfloat16


def _cparams(n_axes):
    return pltpu.CompilerParams(
        dimension_semantics=("arbitrary",) * n_axes,
        vmem_limit_bytes=VMEM_LIMIT,
    )


def _const_spec(shape):
    zeros = (0,) * len(shape)
    return pl.BlockSpec(shape, lambda *_: zeros, pipeline_mode=pl.Buffered(1))


def _rms(x, g):
    return x * lax.rsqrt(jnp.mean(x * x, axis=-1, keepdims=True) + EPS) * g


def _keep_rows(t, rows):
    r = lax.broadcasted_iota(jnp.int32, (rows, 1), 0)
    return jnp.logical_or(t > 0, r >= PADF)


def _dot(a, b):
    return jnp.dot(a, b, preferred_element_type=F32)


def _dot_nt(a, b):
    return lax.dot_general(a, b, (((1,), (1,)), ((), ())), preferred_element_type=F32)


def _dot_tn(a, b):
    return lax.dot_general(a, b, (((0,), (0,)), ((), ())), preferred_element_type=F32)


def _norm_matmul_kernel(*refs, n_out, scales, transposed, n_chunk):
    h_ref, g_ref = refs[0], refs[1]
    w_refs = refs[2:2 + n_out]
    o_refs = refs[2 + n_out:2 + 2 * n_out]
    xn = _rms(h_ref[0], g_ref[...]).astype(BF16)
    for w_ref, o_ref, scale, tr in zip(w_refs, o_refs, scales, transposed):
        n = w_ref.shape[-1]
        step = min(n_chunk, n)
        for n0 in range(0, n, step):
            acc = _dot(xn, w_ref[:, n0:n0 + step])
            if scale != 1.0:
                acc = acc * scale
            if tr:
                o_ref[0, n0:n0 + step, :] = acc.T.astype(o_ref.dtype)
            else:
                o_ref[0, :, n0:n0 + step] = acc.astype(o_ref.dtype)


def _norm_matmul(h, g, ws, out_dtypes, scales, transposed=None):
    b, lp, d = h.shape
    tm = ROW_TILE
    transposed = tuple(transposed) if transposed is not None else (False,) * len(ws)
    kern = functools.partial(_norm_matmul_kernel, n_out=len(ws), scales=tuple(scales),
                             transposed=transposed, n_chunk=512)
    out_specs, out_shape = [], []
    for w, dt, tr in zip(ws, out_dtypes, transposed):
        n = w.shape[1]
        if tr:
            out_specs.append(pl.BlockSpec((1, n, tm), lambda i, t: (i, 0, t)))
            out_shape.append(jax.ShapeDtypeStruct((b, n, lp), dt))
        else:
            out_specs.append(pl.BlockSpec((1, tm, n), lambda i, t: (i, t, 0)))
            out_shape.append(jax.ShapeDtypeStruct((b, lp, n), dt))
    return pl.pallas_call(
        kern,
        grid=(b, lp // tm),
        in_specs=[pl.BlockSpec((1, tm, d), lambda i, t: (i, t, 0)), _const_spec((1, d))]
        + [_const_spec(w.shape) for w in ws],
        out_specs=out_specs,
        out_shape=out_shape,
        compiler_params=_cparams(2),
        name="norm_matmul",
    )(h, g.reshape(1, d), *ws)


def _bdot(a, b):
    return _dot(a.astype(BF16), b.astype(BF16))


def _tri_inverse(mats, row, col):
    n = mats[0].shape[0]
    eye = (row == col).astype(F32)

    def same_block(s):
        sh = s.bit_length() - 1
        return lax.shift_right_logical(row, sh) == lax.shift_right_logical(col, sh)

    nb = [jnp.where(same_block(8), -a, 0.0).astype(BF16) for a in mats]
    n2 = [_dot(x, x) for x in nb]
    p = [eye + x.astype(F32) for x in nb]
    p = [x + _bdot(x, y) for x, y in zip(p, n2)]
    n4 = [_bdot(y, y) for y in n2]
    p = [x + _bdot(x, y) for x, y in zip(p, n4)]
    s = 8
    while s < n:
        pick = jnp.logical_and(same_block(2 * s), jnp.logical_not(same_block(s)))
        off = [jnp.where(pick, a, 0.0).astype(BF16) for a in mats]
        pb = [x.astype(BF16) for x in p]
        y = [_dot(o, x) for o, x in zip(off, pb)]
        p = [x - _bdot(xb, yy) for x, xb, yy in zip(p, pb, y)]
        s *= 2
    return p


def _gdn_kernel(x_ref, gt_ref, cw_ref, alog_ref, dtb_ref, og_ref, o_ref, state_ref, carry_ref):
    t = pl.program_id(1)
    c = CHUNK
    nh = GDN_HEADS
    dk = LANE
    qk_w = nh * dk

    @pl.when(t == 0)
    def _():
        state_ref[...] = jnp.zeros_like(state_ref)
        carry_ref[...] = jnp.zeros_like(carry_ref)

    row = lax.broadcasted_iota(jnp.int32, (c, c), 0)
    col = lax.broadcasted_iota(jnp.int32, (c, c), 1)
    causal = col <= row
    strict = col < row
    tril = causal.astype(F32)
    triu = (row <= col).astype(F32)
    heads = range(nh)
    state = [state_ref[h] for h in heads]
    carry = carry_ref[...]
    for r0 in range(0, x_ref.shape[1], c):
        state, carry = _gdn_chunk(x_ref, gt_ref, cw_ref, alog_ref, dtb_ref, og_ref, o_ref, r0,
                                  _keep_rows(t, c) if r0 == 0 else None, state, carry,
                                  row, col, causal, strict, tril, triu)
    for h in heads:
        state_ref[h] = state[h]
    carry_ref[...] = carry


def _gdn_chunk(x_ref, gt_ref, cw_ref, alog_ref, dtb_ref, og_ref, o_ref, r0, keep, state, carry,
               row, col, causal, strict, tril, triu):
    c = CHUNK
    nh = GDN_HEADS
    dk = LANE
    qk_w = nh * dk
    heads = range(nh)

    x = x_ref[0, r0:r0 + c, 0:3 * qk_w].astype(F32)
    ext = jnp.concatenate([carry, x], axis=0)
    cw = cw_ref[...]
    conv = (cw[3:4] * x + cw[2:3] * ext[7:7 + c] + cw[1:2] * ext[6:6 + c] + cw[0:1] * ext[5:5 + c])
    carry = x[c - 8:c]
    qkv = conv * jax.nn.sigmoid(conv)

    gts = gt_ref[0, r0:r0 + c, :]
    beta_all = jax.nn.sigmoid(gts)
    g_all = -jnp.exp(alog_ref[...]) * jax.nn.softplus(gts + dtb_ref[...])
    if keep is not None:
        beta_all = jnp.where(keep, beta_all, 0.0)
        g_all = jnp.where(keep, g_all, 0.0)

    gc_all = jnp.dot(tril, g_all, preferred_element_type=F32, precision=lax.Precision.HIGHEST)
    gct_all = jnp.dot(g_all.T, triu, preferred_element_type=F32, precision=lax.Precision.HIGHEST)

    qn, kn, vb, kb, decay, egc, gcol = [], [], [], [], [], [], []
    for h in heads:
        qh = qkv[:, h * dk:(h + 1) * dk]
        kh = qkv[:, qk_w + h * dk:qk_w + (h + 1) * dk]
        vh = qkv[:, 2 * qk_w + h * dk:2 * qk_w + (h + 1) * dk]
        qn.append(qh * lax.rsqrt(jnp.sum(qh * qh, axis=-1, keepdims=True) + EPS) * (dk ** -0.5))
        kn.append(kh * lax.rsqrt(jnp.sum(kh * kh, axis=-1, keepdims=True) + EPS))
        beta = beta_all[:, h:h + 1]
        gcol.append(gc_all[:, nh + h:nh + h + 1])
        grow = gct_all[nh + h:nh + h + 1, :]
        decay.append(jnp.where(causal, jnp.exp(gcol[h] - grow), 0.0))
        egc.append(jnp.exp(gcol[h]))
        kb.append(kn[h] * beta)
        vb.append(vh * beta)
    kq = [_dot_nt(jnp.concatenate([kb[h], qn[h]], axis=0).astype(BF16), kn[h].astype(BF16))
          for h in heads]
    tinv = _tri_inverse([jnp.where(strict, kq[h][:c] * decay[h], 0.0) for h in heads], row, col)
    uw = [_bdot(tinv[h], jnp.concatenate([vb[h], kb[h] * egc[h]], axis=1)) for h in heads]
    ws_qs = [_bdot(jnp.concatenate([uw[h][:, dk:], qn[h] * egc[h]], axis=0), state[h]) for h in heads]
    v_new = [uw[h][:, :dk] - ws_qs[h][:c] for h in heads]
    o = [ws_qs[h][c:] + _bdot(kq[h][c:] * decay[h], v_new[h]) for h in heads]
    g_last = [gcol[h][c - 1:c] for h in heads]
    kv = [_dot_tn((kn[h] * jnp.exp(g_last[h] - gcol[h])).astype(BF16), v_new[h].astype(BF16)) for h in heads]
    for h in heads:
        z = x_ref[0, r0:r0 + c, 3 * qk_w + h * dk:3 * qk_w + (h + 1) * dk].astype(F32)
        on = _rms(o[h], og_ref[...])
        o_ref[0, r0:r0 + c, h * dk:(h + 1) * dk] = (on * (z * jax.nn.sigmoid(z))).astype(o_ref.dtype)
    return [state[h] * jnp.exp(g_last[h]) + kv[h] for h in heads], carry


def _gdn_core(proj, gates, conv_w, alog_row, dtb_row, o_gain):
    b, lp, width = proj.shape
    v_w = GDN_HEADS * LANE
    c = ROW_TILE
    return pl.pallas_call(
        _gdn_kernel,
        grid=(b, lp // c),
        in_specs=[pl.BlockSpec((1, c, width), lambda i, t: (i, t, 0)),
                  pl.BlockSpec((1, c, LANE), lambda i, t: (i, t, 0)),
                  _const_spec(conv_w.shape), _const_spec((1, LANE)), _const_spec((1, LANE)),
                  _const_spec((1, LANE))],
        out_specs=pl.BlockSpec((1, c, v_w), lambda i, t: (i, t, 0)),
        out_shape=jax.ShapeDtypeStruct((b, lp, v_w), BF16),
        scratch_shapes=[pltpu.VMEM((GDN_HEADS, LANE, LANE), F32),
                        pltpu.VMEM((8, 3 * v_w), F32)],
        compiler_params=_cparams(2),
        name="gdn_core",
    )(proj, gates, conv_w, alog_row, dtb_row, o_gain.reshape(1, LANE))


def _diff_attn_kernel(q_ref, k_ref, vt_ref, bias_ref, lq1_ref, lk1_ref, lq2_ref, lk2_ref, sub_ref, o_in_ref,
                      o_ref, *scratch, tile, lam_init):
    del o_in_ref
    per_head = len(scratch) // HEADS_PER_STEP
    lam = (jnp.exp(jnp.sum(lq1_ref[...] * lk1_ref[...], axis=-1, keepdims=True))
           - jnp.exp(jnp.sum(lq2_ref[...] * lk2_ref[...], axis=-1, keepdims=True)) + lam_init)
    outs = [_attn_head(q_ref, k_ref, vt_ref, bias_ref, sub_ref, lam, hh,
                       scratch[hh * per_head:(hh + 1) * per_head], tile, lam_init)
            for hh in range(HEADS_PER_STEP)]
    for hh, out in enumerate(outs):
        o_ref[0, :, hh * LANE:(hh + 1) * LANE] = out


def _attn_head(q_ref, k_ref, vt_ref, bias_ref, sub_ref, lam, hh, scratch, tile, lam_init):
    qq_ref, *chunk_refs = scratch
    hs = slice(hh * LANE, (hh + 1) * LANE)
    tq = q_ref.shape[1]
    tk = KV_TILE
    kv_len = k_ref.shape[1]
    dh = LANE // 2
    qc = Q_CHUNK
    n_chunks = 2 * tq // qc
    s_refs, m_refs, acc_refs = (chunk_refs[n::3] for n in range(3))
    q_lo = tile * tq
    n_kv = -(-(q_lo + tq) // tk)

    q = q_ref[0, :, hs]
    lane = lax.broadcasted_iota(jnp.int32, (1, LANE), 1)
    zero = jnp.zeros_like(q)
    qq_ref[0:tq, :] = jnp.where(lane < dh, q, zero)
    qq_ref[tq:2 * tq, :] = jnp.where(lane >= dh, q, zero)

    bias = bias_ref[hh]
    step_bias = bias[1:2, 0:1] * float(tk)
    krel = lax.broadcasted_iota(jnp.int32, (tk, qc), 0)
    ones_rows = jnp.ones((SUM_ROWS, tk), BF16)
    c2 = lax.broadcasted_iota(jnp.int32, (1, 2 * tq), 1)
    qpos = q_lo + jnp.where(c2 >= tq, c2 - tq, c2)

    def block_start(j):
        return min(j * tk, kv_len - tk)

    def scores(j, ci):
        start = block_start(j)
        return _dot_nt(k_ref[0, start:start + tk, hs], qq_ref[ci * qc:(ci + 1) * qc, :])

    for ci in range(n_chunks):
        s_refs[ci][...] = scores(0, ci)
        m_refs[ci][...] = jnp.full_like(m_refs[ci], NEG)
        acc_refs[ci][...] = jnp.zeros_like(acc_refs[ci])

    for j in range(n_kv):
        start = block_start(j)
        diagonal = j * tk + tk - 1 > q_lo or start != j * tk
        vt = jnp.concatenate([vt_ref[0, hs, start:start + tk], ones_rows], axis=0)
        c = step_bias * (start / tk)
        for ci in range(n_chunks):
            s = s_refs[ci][...] + bias
            if j + 1 < n_kv:
                s_next = scores(j + 1, ci)
            if diagonal:
                kpos = start + krel
                valid = jnp.logical_and(kpos <= qpos[:, ci * qc:(ci + 1) * qc], kpos >= max(j * tk, PADF))
                s = jnp.where(valid, s, NEG)
            elif j == 0:
                s = jnp.where(krel >= PADF, s, NEG)
            m_prev = m_refs[ci][...]
            m_new = jnp.maximum(m_prev, jnp.max(s, axis=0, keepdims=True) + c)
            alpha = jnp.exp2(m_prev - m_new)
            p = jnp.exp2((s - (m_new - c)).astype(BF16))
            acc_refs[ci][...] = alpha * acc_refs[ci][...] + _dot(vt, p)
            m_refs[ci][...] = m_new
            if j + 1 < n_kv:
                s_refs[ci][...] = s_next

    accs = [acc_refs[ci][...] for ci in range(n_chunks)]
    o = jnp.concatenate([a[0:LANE] / a[LANE:LANE + 1] for a in accs], axis=1)
    od = o[:, :tq] - lam * o[:, tq:]
    on = od * lax.rsqrt(jnp.mean(od * od, axis=0, keepdims=True) + EPS) * sub_ref[...]
    return (on * (1.0 - lam_init)).T.astype(BF16)


def _diff_attn(q, k, vt, bias, lq1, lk1, lq2, lk2, subln, lam_init):
    b, lp, width = q.shape
    nh = DIFF_HEADS
    hps = HEADS_PER_STEP
    tq = ROW_TILE
    tk = KV_TILE
    dh = lq1.shape[-1]
    vec = lambda a: a.reshape(1, dh).astype(F32)
    head_scratch = [pltpu.VMEM((2 * tq, LANE), BF16)] + [
        pltpu.VMEM((tk, Q_CHUNK), F32), pltpu.VMEM((1, Q_CHUNK), F32),
        pltpu.VMEM((LANE + SUM_ROWS, Q_CHUNK), F32)] * (2 * tq // Q_CHUNK)
    out = jnp.zeros((b, lp, width), BF16)
    for tile in range(lp // tq):
        kv_len = min(-(-(tile + 1) * tq // tk) * tk, lp)
        out = pl.pallas_call(
            functools.partial(_diff_attn_kernel, tile=tile, lam_init=lam_init),
            grid=(b, nh // hps),
            in_specs=[pl.BlockSpec((1, tq, hps * LANE), lambda bi, h, tile=tile: (bi, tile, h)),
                      pl.BlockSpec((1, kv_len, hps * LANE), lambda bi, h: (bi, 0, h)),
                      pl.BlockSpec((1, hps * LANE, kv_len), lambda bi, h: (bi, h, 0)),
                      pl.BlockSpec((hps, tk, Q_CHUNK), lambda bi, h: (h, 0, 0)),
                      _const_spec((1, dh)), _const_spec((1, dh)), _const_spec((1, dh)), _const_spec((1, dh)),
                      _const_spec((LANE, 1)),
                      pl.BlockSpec(memory_space=pl.ANY)],
            out_specs=pl.BlockSpec((1, tq, hps * LANE), lambda bi, h, tile=tile: (bi, tile, h)),
            out_shape=jax.ShapeDtypeStruct((b, lp, width), BF16),
            scratch_shapes=head_scratch * hps,
            input_output_aliases={9: 0},
            compiler_params=_cparams(2),
            name="diff_attn",
        )(q, k, vt, bias, vec(lq1), vec(lk1), vec(lq2), vec(lk2), subln.reshape(LANE, 1).astype(F32), out)
    return out


def _ffn_kernel(h_ref, y_ref, wo_ref, g_ref, wg_ref, wu_ref, cg_ref, cu_ref, wd_ref, fg_ref, o_ref,
                xn_ref, acc_ref, carry_g_ref, carry_u_ref, ug_a, uu_a, ug_b, uu_b, *, final):
    t = pl.program_id(1)
    tm = h_ref.shape[1]
    nf = wg_ref.shape[0]
    slot_a = (ug_a, uu_a)
    slot_b = (ug_b, uu_b)

    @pl.when(t == 0)
    def _():
        carry_g_ref[...] = jnp.zeros_like(carry_g_ref)
        carry_u_ref[...] = jnp.zeros_like(carry_u_ref)

    x = jnp.where(_keep_rows(t, tm), h_ref[0] + _dot(y_ref[0], wo_ref[...]), 0.0)
    xn_ref[...] = _rms(x, g_ref[...]).astype(BF16)
    acc_ref[...] = x

    rc = LANE

    def up_gate(f, slot):
        slot[0][8:8 + tm, :] = _dot(xn_ref[...], wg_ref[f])

    def up_up(f, slot):
        slot[1][8:8 + tm, :] = _dot(xn_ref[...], wu_ref[f])

    def load_halo(f, slot):
        for u_ref, carry_ref in ((slot[0], carry_g_ref), (slot[1], carry_u_ref)):
            u_ref[0:8, :] = carry_ref[f]
            carry_ref[f] = u_ref[tm:tm + 8, :]

    def conv(u_ref, cw, r0):
        return (cw[2:3] * u_ref[8 + r0:8 + r0 + rc, :] + cw[1:2] * u_ref[7 + r0:7 + r0 + rc, :]
                + cw[0:1] * u_ref[6 + r0:6 + r0 + rc, :])

    def act_down(f, slot, r0):
        gate = conv(slot[0], cg_ref[f], r0)
        up = conv(slot[1], cu_ref[f], r0)
        act = (gate * jax.nn.sigmoid(gate) * up).astype(BF16)
        acc_ref[r0:r0 + rc, :] += _dot(act, wd_ref[f])

    def stage(f, cur, other, last=False):
        load_halo(f, cur)
        if not last:
            up_gate(f + 1, other)
        act_down(f, cur, 0)
        if not last:
            up_up(f + 1, other)
        for r0 in range(rc, tm, rc):
            act_down(f, cur, r0)

    up_gate(0, slot_a)
    up_up(0, slot_a)
    for f in range(0, nf - 1, 2):
        stage(f, slot_a, slot_b)
        stage(f + 1, slot_b, slot_a)
    stage(nf - 1, slot_a, slot_b, last=True)
    out = acc_ref[...]
    if final:
        out = _rms(out, fg_ref[...])
    o_ref[0] = jnp.where(_keep_rows(t, tm), out, 0.0)


def _ffn(h, y, wo, g, wg, wu, cg, cu, wd, final_gain=None):
    b, lp, d = h.shape
    tm = ROW_TILE
    nf, _, tf = wg.shape
    final = final_gain is not None
    fg = (final_gain if final else g).reshape(1, d)
    return pl.pallas_call(
        functools.partial(_ffn_kernel, final=final),
        grid=(b, lp // tm),
        in_specs=[pl.BlockSpec((1, tm, d), lambda i, t: (i, t, 0)),
                  pl.BlockSpec((1, tm, y.shape[2]), lambda i, t: (i, t, 0)),
                  _const_spec(wo.shape), _const_spec((1, d)),
                  _const_spec(wg.shape), _const_spec(wu.shape), _const_spec(cg.shape),
                  _const_spec(cu.shape), _const_spec(wd.shape), _const_spec((1, d))],
        out_specs=pl.BlockSpec((1, tm, d), lambda i, t: (i, t, 0)),
        out_shape=jax.ShapeDtypeStruct(h.shape, h.dtype),
        scratch_shapes=[pltpu.VMEM((tm, d), BF16), pltpu.VMEM((tm, d), F32),
                        pltpu.VMEM((nf, 8, tf), F32), pltpu.VMEM((nf, 8, tf), F32)]
        + [pltpu.VMEM((tm + 8, tf), F32)] * 4,
        input_output_aliases={0: 0},
        compiler_params=_cparams(2),
        name="conv_ffn",
    )(h, y, wo, g.reshape(1, d), wg, wu, cg, cu, wd, fg)


def _lane_row(vals, offset):
    return jnp.zeros((1, LANE), F32).at[0, offset:offset + vals.shape[0]].set(vals.astype(F32))


def kernel(x, meta_tokens, a_norm, a_w_in, a_conv, a_log, a_dt_bias, a_onorm, a_w_o, kv_norm, w_kv, lambda_k1, lambda_k2, b_norm, b_w_q, b_lambda_q1, b_lambda_q2, b_subln, b_w_o, ffn_norm, ffn_w_up, ffn_conv, ffn_w_down, final_norm):
    bn, seq, d = x.shape
    n_a = a_w_in.shape[0]
    n_b = b_w_q.shape[0]
    depth = n_a + n_b
    nh = GDN_HEADS
    qkvz_w = 4 * nh * LANE
    assert d == nh * LANE and seq % LANE == 0 and (LANE + seq) % ROW_TILE == 0
    d_ff = ffn_w_down.shape[1]
    assert d_ff % FF_TILE == 0
    nf = d_ff // FF_TILE
    assert nf >= 3 and nf % 2 == 1

    meta = jnp.broadcast_to(meta_tokens.astype(x.dtype)[None], (bn, N_META, d))
    h = jnp.concatenate([jnp.zeros((bn, PADF, d), x.dtype), meta, x], axis=1)

    def ffn_layer(h, mixer_out, w_o, layer):
        w_up = ffn_w_up[layer].astype(BF16).reshape(d, 2, nf, FF_TILE).transpose(1, 2, 0, 3)
        cw = ffn_conv[layer].astype(F32).reshape(FFN_CONV, 2, nf, FF_TILE).transpose(1, 2, 0, 3)
        w_dn = ffn_w_down[layer].astype(BF16).reshape(nf, FF_TILE, d)
        return _ffn(h, mixer_out, w_o.astype(BF16), ffn_norm[layer], w_up[0], w_up[1], cw[0], cw[1], w_dn,
                    final_gain=final_norm if layer == depth - 1 else None)

    for layer in range(n_a):
        w_in = a_w_in[layer]
        w_main = w_in[:, :qkvz_w].astype(BF16)
        w_gate = jnp.pad(w_in[:, qkvz_w:], ((0, 0), (0, LANE - 2 * nh))).astype(BF16)
        proj, gates = _norm_matmul(h, a_norm[layer], [w_main, w_gate], [BF16, F32], [1.0, 1.0])
        o = _gdn_core(proj, gates, a_conv[layer].astype(F32), _lane_row(a_log[layer], nh),
                      _lane_row(a_dt_bias[layer], nh), a_onorm[layer].astype(F32))
        h = ffn_layer(h, o, a_w_o[layer], layer)

    diff_w = DIFF_HEADS * LANE
    k_sh, vt_sh = _norm_matmul(h, kv_norm, [w_kv[:, :diff_w].astype(BF16), w_kv[:, diff_w:].astype(BF16)],
                               [BF16, BF16], [1.0, 1.0], transposed=[False, True])
    slopes = jnp.exp2(-(8.0 / DIFF_HEADS) * jnp.arange(1, DIFF_HEADS + 1, dtype=F32))
    bias_tab = (slopes * LOG2E)[:, None, None] * jnp.arange(KV_TILE, dtype=F32)[None, :, None]
    bias_tab = jnp.broadcast_to(bias_tab, (DIFF_HEADS, KV_TILE, Q_CHUNK))
    dh = lambda_k1.shape[0]
    for j in range(n_b):
        layer = n_a + j
        lam_init = 0.8 - 0.6 * math.exp(-0.3 * layer)
        (q,) = _norm_matmul(h, b_norm[j], [b_w_q[j].astype(BF16)], [BF16], [dh ** -0.5 * LOG2E])
        o = _diff_attn(q, k_sh, vt_sh, bias_tab, b_lambda_q1[j], lambda_k1, b_lambda_q2[j], lambda_k2,
                       b_subln[j], lam_init)
        h = ffn_layer(h, o, b_w_o[j], layer)

    return h[:, LANE:]
```

```python
import functools
import math

import jax
import jax.numpy as jnp
from jax import lax
from jax.experimental import pallas as pl
from jax.experimental.pallas import tpu as pltpu

N_META = 16
GDN_HEADS = 8
GDN_CONV = 4
DIFF_HEADS = 8
FFN_CONV = 3
EPS = 1e-6

LANE = 128
PADF = LANE - N_META
CHUNK = 128
ROW_TILE = 384
KV_TILE = 512
Q_CHUNK = 256
FF_TILE = 256
HEADS_PER_STEP = 4
SUM_ROWS = 16
LOG2E = math.log2(math.e)
NEG = -1e30
VMEM_LIMIT = 56 * 1024 * 1024

F32 = jnp.float32
BF16 = jnp.bfloat16


def _cparams(n_axes):
    return pltpu.CompilerParams(
        dimension_semantics=("arbitrary",) * n_axes,
        vmem_limit_bytes=VMEM_LIMIT,
    )


def _const_spec(shape):
    zeros = (0,) * len(shape)
    return pl.BlockSpec(shape, lambda *_: zeros, pipeline_mode=pl.Buffered(1))


def _rms(x, g):
    return x * lax.rsqrt(jnp.mean(x * x, axis=-1, keepdims=True) + EPS) * g


def _keep_rows(t, rows):
    r = lax.broadcasted_iota(jnp.int32, (rows, 1), 0)
    return jnp.logical_or(t > 0, r >= PADF)


def _dot(a, b):
    return jnp.dot(a, b, preferred_element_type=F32)


def _dot_nt(a, b):
    return lax.dot_general(a, b, (((1,), (1,)), ((), ())), preferred_element_type=F32)


def _dot_tn(a, b):
    return lax.dot_general(a, b, (((0,), (0,)), ((), ())), preferred_element_type=F32)


def _norm_matmul_kernel(*refs, n_out, scales, transposed, n_chunk):
    h_ref, g_ref = refs[0], refs[1]
    w_refs = refs[2:2 + n_out]
    o_refs = refs[2 + n_out:2 + 2 * n_out]
    xn = _rms(h_ref[0], g_ref[...]).astype(BF16)
    for w_ref, o_ref, scale, tr in zip(w_refs, o_refs, scales, transposed):
        n = w_ref.shape[-1]
        step = min(n_chunk, n)
        for n0 in range(0, n, step):
            acc = _dot(xn, w_ref[:, n0:n0 + step])
            if scale != 1.0:
                acc = acc * scale
            if tr:
                o_ref[0, n0:n0 + step, :] = acc.T.astype(o_ref.dtype)
            else:
                o_ref[0, :, n0:n0 + step] = acc.astype(o_ref.dtype)


def _norm_matmul(h, g, ws, out_dtypes, scales, transposed=None):
    b, lp, d = h.shape
    tm = ROW_TILE
    transposed = tuple(transposed) if transposed is not None else (False,) * len(ws)
    kern = functools.partial(_norm_matmul_kernel, n_out=len(ws), scales=tuple(scales),
                             transposed=transposed, n_chunk=512)
    out_specs, out_shape = [], []
    for w, dt, tr in zip(ws, out_dtypes, transposed):
        n = w.shape[1]
        if tr:
            out_specs.append(pl.BlockSpec((1, n, tm), lambda i, t: (i, 0, t)))
            out_shape.append(jax.ShapeDtypeStruct((b, n, lp), dt))
        else:
            out_specs.append(pl.BlockSpec((1, tm, n), lambda i, t: (i, t, 0)))
            out_shape.append(jax.ShapeDtypeStruct((b, lp, n), dt))
    return pl.pallas_call(
        kern,
        grid=(b, lp // tm),
        in_specs=[pl.BlockSpec((1, tm, d), lambda i, t: (i, t, 0)), _const_spec((1, d))]
        + [_const_spec(w.shape) for w in ws],
        out_specs=out_specs,
        out_shape=out_shape,
        compiler_params=_cparams(2),
        name="norm_matmul",
    )(h, g.reshape(1, d), *ws)


def _bdot(a, b):
    return _dot(a.astype(BF16), b.astype(BF16))


def _tri_inverse(mats, row, col):
    n = mats[0].shape[0]
    eye = (row == col).astype(F32)

    def same_block(s):
        sh = s.bit_length() - 1
        return lax.shift_right_logical(row, sh) == lax.shift_right_logical(col, sh)

    nb = [jnp.where(same_block(8), -a, 0.0).astype(BF16) for a in mats]
    n2 = [_dot(x, x) for x in nb]
    p = [eye + x.astype(F32) for x in nb]
    p = [x + _bdot(x, y) for x, y in zip(p, n2)]
    n4 = [_bdot(y, y) for y in n2]
    p = [x + _bdot(x, y) for x, y in zip(p, n4)]
    s = 8
    while s < n:
        pick = jnp.logical_and(same_block(2 * s), jnp.logical_not(same_block(s)))
        off = [jnp.where(pick, a, 0.0).astype(BF16) for a in mats]
        pb = [x.astype(BF16) for x in p]
        y = [_dot(o, x) for o, x in zip(off, pb)]
        p = [x - _bdot(xb, yy) for x, xb, yy in zip(p, pb, y)]
        s *= 2
    return p


def _gdn_kernel(x_ref, gt_ref, cw_ref, alog_ref, dtb_ref, og_ref, o_ref, state_ref, carry_ref):
    t = pl.program_id(1)
    c = CHUNK
    nh = GDN_HEADS
    dk = LANE
    qk_w = nh * dk

    @pl.when(t == 0)
    def _():
        state_ref[...] = jnp.zeros_like(state_ref)
        carry_ref[...] = jnp.zeros_like(carry_ref)

    row = lax.broadcasted_iota(jnp.int32, (c, c), 0)
    col = lax.broadcasted_iota(jnp.int32, (c, c), 1)
    causal = col <= row
    strict = col < row
    tril = causal.astype(F32)
    triu = (row <= col).astype(F32)
    heads = range(nh)
    state = [state_ref[h] for h in heads]
    carry = carry_ref[...]
    for r0 in range(0, x_ref.shape[1], c):
        state, carry = _gdn_chunk(x_ref, gt_ref, cw_ref, alog_ref, dtb_ref, og_ref, o_ref, r0,
                                  _keep_rows(t, c) if r0 == 0 else None, state, carry,
                                  row, col, causal, strict, tril, triu)
    for h in heads:
        state_ref[h] = state[h]
    carry_ref[...] = carry


def _gdn_chunk(x_ref, gt_ref, cw_ref, alog_ref, dtb_ref, og_ref, o_ref, r0, keep, state, carry,
               row, col, causal, strict, tril, triu):
    c = CHUNK
    nh = GDN_HEADS
    dk = LANE
    qk_w = nh * dk
    heads = range(nh)

    x = x_ref[0, r0:r0 + c, 0:3 * qk_w].astype(F32)
    ext = jnp.concatenate([carry, x], axis=0)
    cw = cw_ref[...]
    conv = (cw[3:4] * x + cw[2:3] * ext[7:7 + c] + cw[1:2] * ext[6:6 + c] + cw[0:1] * ext[5:5 + c])
    carry = x[c - 8:c]
    qkv = conv * jax.nn.sigmoid(conv)

    gts = gt_ref[0, r0:r0 + c, :]
    beta_all = jax.nn.sigmoid(gts)
    g_all = -jnp.exp(alog_ref[...]) * jax.nn.softplus(gts + dtb_ref[...])
    if keep is not None:
        beta_all = jnp.where(keep, beta_all, 0.0)
        g_all = jnp.where(keep, g_all, 0.0)

    gc_all = jnp.dot(tril, g_all, preferred_element_type=F32, precision=lax.Precision.HIGHEST)
    gct_all = jnp.dot(g_all.T, triu, preferred_element_type=F32, precision=lax.Precision.HIGHEST)

    qn, kn, vb, kb, decay, egc, gcol = [], [], [], [], [], [], []
    for h in heads:
        qh = qkv[:, h * dk:(h + 1) * dk]
        kh = qkv[:, qk_w + h * dk:qk_w + (h + 1) * dk]
        vh = qkv[:, 2 * qk_w + h * dk:2 * qk_w + (h + 1) * dk]
        qn.append(qh * lax.rsqrt(jnp.sum(qh * qh, axis=-1, keepdims=True) + EPS) * (dk ** -0.5))
        kn.append(kh * lax.rsqrt(jnp.sum(kh * kh, axis=-1, keepdims=True) + EPS))
        beta = beta_all[:, h:h + 1]
        gcol.append(gc_all[:, nh + h:nh + h + 1])
        grow = gct_all[nh + h:nh + h + 1, :]
        decay.append(jnp.where(causal, jnp.exp(gcol[h] - grow), 0.0))
        egc.append(jnp.exp(gcol[h]))
        kb.append(kn[h] * beta)
        vb.append(vh * beta)
    kq = [_dot_nt(jnp.concatenate([kb[h], qn[h]], axis=0).astype(BF16), kn[h].astype(BF16))
          for h in heads]
    tinv = _tri_inverse([jnp.where(strict, kq[h][:c] * decay[h], 0.0) for h in heads], row, col)
    uw = [_bdot(tinv[h], jnp.concatenate([vb[h], kb[h] * egc[h]], axis=1)) for h in heads]
    ws_qs = [_bdot(jnp.concatenate([uw[h][:, dk:], qn[h] * egc[h]], axis=0), state[h]) for h in heads]
    v_new = [uw[h][:, :dk] - ws_qs[h][:c] for h in heads]
    o = [ws_qs[h][c:] + _bdot(kq[h][c:] * decay[h], v_new[h]) for h in heads]
    g_last = [gcol[h][c - 1:c] for h in heads]
    kv = [_dot_tn((kn[h] * jnp.exp(g_last[h] - gcol[h])).astype(BF16), v_new[h].astype(BF16)) for h in heads]
    for h in heads:
        z = x_ref[0, r0:r0 + c, 3 * qk_w + h * dk:3 * qk_w + (h + 1) * dk].astype(F32)
        on = _rms(o[h], og_ref[...])
        o_ref[0, r0:r0 + c, h * dk:(h + 1) * dk] = (on * (z * jax.nn.sigmoid(z))).astype(o_ref.dtype)
    return [state[h] * jnp.exp(g_last[h]) + kv[h] for h in heads], carry


def _gdn_core(proj, gates, conv_w, alog_row, dtb_row, o_gain):
    b, lp, width = proj.shape
    v_w = GDN_HEADS * LANE
    c = ROW_TILE
    return pl.pallas_call(
        _gdn_kernel,
        grid=(b, lp // c),
        in_specs=[pl.BlockSpec((1, c, width), lambda i, t: (i, t, 0)),
                  pl.BlockSpec((1, c, LANE), lambda i, t: (i, t, 0)),
                  _const_spec(conv_w.shape), _const_spec((1, LANE)), _const_spec((1, LANE)),
                  _const_spec((1, LANE))],
        out_specs=pl.BlockSpec((1, c, v_w), lambda i, t: (i, t, 0)),
        out_shape=jax.ShapeDtypeStruct((b, lp, v_w), BF16),
        scratch_shapes=[pltpu.VMEM((GDN_HEADS, LANE, LANE), F32),
                        pltpu.VMEM((8, 3 * v_w), F32)],
        compiler_params=_cparams(2),
        name="gdn_core",
    )(proj, gates, conv_w, alog_row, dtb_row, o_gain.reshape(1, LANE))


def _diff_attn_kernel(q_ref, k_ref, vt_ref, bias_ref, lq1_ref, lk1_ref, lq2_ref, lk2_ref, sub_ref, o_in_ref,
                      o_ref, *scratch, tile, lam_init):
    del o_in_ref
    per_head = len(scratch) // HEADS_PER_STEP
    lam = (jnp.exp(jnp.sum(lq1_ref[...] * lk1_ref[...], axis=-1, keepdims=True))
           - jnp.exp(jnp.sum(lq2_ref[...] * lk2_ref[...], axis=-1, keepdims=True)) + lam_init)
    outs = [_attn_head(q_ref, k_ref, vt_ref, bias_ref, sub_ref, lam, hh,
                       scratch[hh * per_head:(hh + 1) * per_head], tile, lam_init)
            for hh in range(HEADS_PER_STEP)]
    for hh, out in enumerate(outs):
        o_ref[0, :, hh * LANE:(hh + 1) * LANE] = out


def _attn_head(q_ref, k_ref, vt_ref, bias_ref, sub_ref, lam, hh, scratch, tile, lam_init):
    qq_ref, *chunk_refs = scratch
    hs = slice(hh * LANE, (hh + 1) * LANE)
    tq = q_ref.shape[1]
    tk = KV_TILE
    kv_len = k_ref.shape[1]
    dh = LANE // 2
    qc = Q_CHUNK
    n_chunks = 2 * tq // qc
    s_refs, m_refs, acc_refs = (chunk_refs[n::3] for n in range(3))
    q_lo = tile * tq
    n_kv = -(-(q_lo + tq) // tk)

    q = q_ref[0, :, hs]
    lane = lax.broadcasted_iota(jnp.int32, (1, LANE), 1)
    zero = jnp.zeros_like(q)
    qq_ref[0:tq, :] = jnp.where(lane < dh, q, zero)
    qq_ref[tq:2 * tq, :] = jnp.where(lane >= dh, q, zero)

    bias = bias_ref[hh]
    step_bias = bias[1:2, 0:1] * float(tk)
    krel = lax.broadcasted_iota(jnp.int32, (tk, qc), 0)
    ones_rows = jnp.ones((SUM_ROWS, tk), BF16)
    c2 = lax.broadcasted_iota(jnp.int32, (1, 2 * tq), 1)
    qpos = q_lo + jnp.where(c2 >= tq, c2 - tq, c2)

    def block_start(j):
        return min(j * tk, kv_len - tk)

    def scores(j, ci):
        start = block_start(j)
        return _dot_nt(k_ref[0, start:start + tk, hs], qq_ref[ci * qc:(ci + 1) * qc, :])

    for ci in range(n_chunks):
        s_refs[ci][...] = scores(0, ci)
        m_refs[ci][...] = jnp.full_like(m_refs[ci], NEG)
        acc_refs[ci][...] = jnp.zeros_like(acc_refs[ci])

    for j in range(n_kv):
        start = block_start(j)
        diagonal = j * tk + tk - 1 > q_lo or start != j * tk
        vt = jnp.concatenate([vt_ref[0, hs, start:start + tk], ones_rows], axis=0)
        c = step_bias * (start / tk)
        for ci in range(n_chunks):
            s = s_refs[ci][...] + bias
            if j + 1 < n_kv:
                s_next = scores(j + 1, ci)
            if diagonal:
                kpos = start + krel
                valid = jnp.logical_and(kpos <= qpos[:, ci * qc:(ci + 1) * qc], kpos >= max(j * tk, PADF))
                s = jnp.where(valid, s, NEG)
            elif j == 0:
                s = jnp.where(krel >= PADF, s, NEG)
            m_prev = m_refs[ci][...]
            m_new = jnp.maximum(m_prev, jnp.max(s, axis=0, keepdims=True) + c)
            alpha = jnp.exp2(m_prev - m_new)
            p = jnp.exp2((s - (m_new - c)).astype(BF16))
            acc_refs[ci][...] = alpha * acc_refs[ci][...] + _dot(vt, p)
            m_refs[ci][...] = m_new
            if j + 1 < n_kv:
                s_refs[ci][...] = s_next

    accs = [acc_refs[ci][...] for ci in range(n_chunks)]
    o = jnp.concatenate([a[0:LANE] / a[LANE:LANE + 1] for a in accs], axis=1)
    od = o[:, :tq] - lam * o[:, tq:]
    on = od * lax.rsqrt(jnp.mean(od * od, axis=0, keepdims=True) + EPS) * sub_ref[...]
    return (on * (1.0 - lam_init)).T.astype(BF16)


def _diff_attn(q, k, vt, bias, lq1, lk1, lq2, lk2, subln, lam_init):
    b, lp, width = q.shape
    nh = DIFF_HEADS
    hps = HEADS_PER_STEP
    tq = ROW_TILE
    tk = KV_TILE
    dh = lq1.shape[-1]
    vec = lambda a: a.reshape(1, dh).astype(F32)
    head_scratch = [pltpu.VMEM((2 * tq, LANE), BF16)] + [
        pltpu.VMEM((tk, Q_CHUNK), F32), pltpu.VMEM((1, Q_CHUNK), F32),
        pltpu.VMEM((LANE + SUM_ROWS, Q_CHUNK), F32)] * (2 * tq // Q_CHUNK)
    out = jnp.zeros((b, lp, width), BF16)
    for tile in range(lp // tq):
        kv_len = min(-(-(tile + 1) * tq // tk) * tk, lp)
        out = pl.pallas_call(
            functools.partial(_diff_attn_kernel, tile=tile, lam_init=lam_init),
            grid=(b, nh // hps),
            in_specs=[pl.BlockSpec((1, tq, hps * LANE), lambda bi, h, tile=tile: (bi, tile, h)),
                      pl.BlockSpec((1, kv_len, hps * LANE), lambda bi, h: (bi, 0, h)),
                      pl.BlockSpec((1, hps * LANE, kv_len), lambda bi, h: (bi, h, 0)),
                      pl.BlockSpec((hps, tk, Q_CHUNK), lambda bi, h: (h, 0, 0)),
                      _const_spec((1, dh)), _const_spec((1, dh)), _const_spec((1, dh)), _const_spec((1, dh)),
                      _const_spec((LANE, 1)),
                      pl.BlockSpec(memory_space=pl.ANY)],
            out_specs=pl.BlockSpec((1, tq, hps * LANE), lambda bi, h, tile=tile: (bi, tile, h)),
            out_shape=jax.ShapeDtypeStruct((b, lp, width), BF16),
            scratch_shapes=head_scratch * hps,
            input_output_aliases={9: 0},
            compiler_params=_cparams(2),
            name="diff_attn",
        )(q, k, vt, bias, vec(lq1), vec(lk1), vec(lq2), vec(lk2), subln.reshape(LANE, 1).astype(F32), out)
    return out


def _ffn_kernel(h_ref, y_ref, wo_ref, g_ref, wg_ref, wu_ref, cg_ref, cu_ref, wd_ref, fg_ref, o_ref,
                xn_ref, acc_ref, carry_g_ref, carry_u_ref, ug_a, uu_a, ug_b, uu_b, *, final):
    t = pl.program_id(1)
    tm = h_ref.shape[1]
    nf = wg_ref.shape[0]
    slot_a = (ug_a, uu_a)
    slot_b = (ug_b, uu_b)

    @pl.when(t == 0)
    def _():
        carry_g_ref[...] = jnp.zeros_like(carry_g_ref)
        carry_u_ref[...] = jnp.zeros_like(carry_u_ref)

    x = jnp.where(_keep_rows(t, tm), h_ref[0] + _dot(y_ref[0], wo_ref[...]), 0.0)
    xn_ref[...] = _rms(x, g_ref[...]).astype(BF16)
    acc_ref[...] = x

    rc = LANE

    def up_gate(f, slot):
        slot[0][8:8 + tm, :] = _dot(xn_ref[...], wg_ref[f])

    def up_up(f, slot):
        slot[1][8:8 + tm, :] = _dot(xn_ref[...], wu_ref[f])

    def load_halo(f, slot):
        for u_ref, carry_ref in ((slot[0], carry_g_ref), (slot[1], carry_u_ref)):
            u_ref[0:8, :] = carry_ref[f]
            carry_ref[f] = u_ref[tm:tm + 8, :]

    def conv(u_ref, cw, r0):
        return (cw[2:3] * u_ref[8 + r0:8 + r0 + rc, :] + cw[1:2] * u_ref[7 + r0:7 + r0 + rc, :]
                + cw[0:1] * u_ref[6 + r0:6 + r0 + rc, :])

    def act_down(f, slot, r0):
        gate = conv(slot[0], cg_ref[f], r0)
        up = conv(slot[1], cu_ref[f], r0)
        act = (gate * jax.nn.sigmoid(gate) * up).astype(BF16)
        acc_ref[r0:r0 + rc, :] += _dot(act, wd_ref[f])

    def stage(f, cur, other, last=False):
        load_halo(f, cur)
        if not last:
            up_gate(f + 1, other)
        act_down(f, cur, 0)
        if not last:
            up_up(f + 1, other)
        for r0 in range(rc, tm, rc):
            act_down(f, cur, r0)

    up_gate(0, slot_a)
    up_up(0, slot_a)
    for f in range(0, nf - 1, 2):
        stage(f, slot_a, slot_b)
        stage(f + 1, slot_b, slot_a)
    stage(nf - 1, slot_a, slot_b, last=True)
    out = acc_ref[...]
    if final:
        out = _rms(out, fg_ref[...])
    o_ref[0] = jnp.where(_keep_rows(t, tm), out, 0.0)


def _ffn(h, y, wo, g, wg, wu, cg, cu, wd, final_gain=None):
    b, lp, d = h.shape
    tm = ROW_TILE
    nf, _, tf = wg.shape
    final = final_gain is not None
    fg = (final_gain if final else g).reshape(1, d)
    return pl.pallas_call(
        functools.partial(_ffn_kernel, final=final),
        grid=(b, lp // tm),
        in_specs=[pl.BlockSpec((1, tm, d), lambda i, t: (i, t, 0)),
                  pl.BlockSpec((1, tm, y.shape[2]), lambda i, t: (i, t, 0)),
                  _const_spec(wo.shape), _const_spec((1, d)),
                  _const_spec(wg.shape), _const_spec(wu.shape), _const_spec(cg.shape),
                  _const_spec(cu.shape), _const_spec(wd.shape), _const_spec((1, d))],
        out_specs=pl.BlockSpec((1, tm, d), lambda i, t: (i, t, 0)),
        out_shape=jax.ShapeDtypeStruct(h.shape, h.dtype),
        scratch_shapes=[pltpu.VMEM((tm, d), BF16), pltpu.VMEM((tm, d), F32),
                        pltpu.VMEM((nf, 8, tf), F32), pltpu.VMEM((nf, 8, tf), F32)]
        + [pltpu.VMEM((tm + 8, tf), F32)] * 4,
        input_output_aliases={0: 0},
        compiler_params=_cparams(2),
        name="conv_ffn",
    )(h, y, wo, g.reshape(1, d), wg, wu, cg, cu, wd, fg)


def _lane_row(vals, offset):
    return jnp.zeros((1, LANE), F32).at[0, offset:offset + vals.shape[0]].set(vals.astype(F32))


def kernel(x, meta_tokens, a_norm, a_w_in, a_conv, a_log, a_dt_bias, a_onorm, a_w_o, kv_norm, w_kv, lambda_k1, lambda_k2, b_norm, b_w_q, b_lambda_q1, b_lambda_q2, b_subln, b_w_o, ffn_norm, ffn_w_up, ffn_conv, ffn_w_down, final_norm):
    bn, seq, d = x.shape
    n_a = a_w_in.shape[0]
    n_b = b_w_q.shape[0]
    depth = n_a + n_b
    nh = GDN_HEADS
    qkvz_w = 4 * nh * LANE
    assert d == nh * LANE and seq % LANE == 0 and (LANE + seq) % ROW_TILE == 0
    d_ff = ffn_w_down.shape[1]
    assert d_ff % FF_TILE == 0
    nf = d_ff // FF_TILE
    assert nf >= 3 and nf % 2 == 1

    meta = jnp.broadcast_to(meta_tokens.astype(x.dtype)[None], (bn, N_META, d))
    h = jnp.concatenate([jnp.zeros((bn, PADF, d), x.dtype), meta, x], axis=1)

    def ffn_layer(h, mixer_out, w_o, layer):
        w_up = ffn_w_up[layer].astype(BF16).reshape(d, 2, nf, FF_TILE).transpose(1, 2, 0, 3)
        cw = ffn_conv[layer].astype(F32).reshape(FFN_CONV, 2, nf, FF_TILE).transpose(1, 2, 0, 3)
        w_dn = ffn_w_down[layer].astype(BF16).reshape(nf, FF_TILE, d)
        return _ffn(h, mixer_out, w_o.astype(BF16), ffn_norm[layer], w_up[0], w_up[1], cw[0], cw[1], w_dn,
                    final_gain=final_norm if layer == depth - 1 else None)

    for layer in range(n_a):
        w_in = a_w_in[layer]
        w_main = w_in[:, :qkvz_w].astype(BF16)
        w_gate = jnp.pad(w_in[:, qkvz_w:], ((0, 0), (0, LANE - 2 * nh))).astype(BF16)
        proj, gates = _norm_matmul(h, a_norm[layer], [w_main, w_gate], [BF16, F32], [1.0, 1.0])
        o = _gdn_core(proj, gates, a_conv[layer].astype(F32), _lane_row(a_log[layer], nh),
                      _lane_row(a_dt_bias[layer], nh), a_onorm[layer].astype(F32))
        h = ffn_layer(h, o, a_w_o[layer], layer)

    diff_w = DIFF_HEADS * LANE
    k_sh, vt_sh = _norm_matmul(h, kv_norm, [w_kv[:, :diff_w].astype(BF16), w_kv[:, diff_w:].astype(BF16)],
                               [BF16, BF16], [1.0, 1.0], transposed=[False, True])
    slopes = jnp.exp2(-(8.0 / DIFF_HEADS) * jnp.arange(1, DIFF_HEADS + 1, dtype=F32))
    bias_tab = (slopes * LOG2E)[:, None, None] * jnp.arange(KV_TILE, dtype=F32)[None, :, None]
    bias_tab = jnp.broadcast_to(bias_tab, (DIFF_HEADS, KV_TILE, Q_CHUNK))
    dh = lambda_k1.shape[0]
    for j in range(n_b):
        layer = n_a + j
        lam_init = 0.8 - 0.6 * math.exp(-0.3 * layer)
        (q,) = _norm_matmul(h, b_norm[j], [b_w_q[j].astype(BF16)], [BF16], [dh ** -0.5 * LOG2E])
        o = _diff_attn(q, k_sh, vt_sh, bias_tab, b_lambda_q1[j], lambda_k1, b_lambda_q2[j], lambda_k2,
                       b_subln[j], lam_init)
        h = ffn_layer(h, o, b_w_o[j], layer)

    return h[:, LANE:]
```
